```python
import math
import jax, jax.numpy as jnp
from jax import lax
import numpy as np

D_MODEL = 1024
BATCH = 4
SEQ = 8192
DEPTH = 2

HEAD_DIM = 64
NSA_HEADS = 4
NSA_W = NSA_HEADS * HEAD_DIM
NSA_CMP_LEN = 32
NSA_CMP_STRIDE = 16
NSA_CMP_HIDDEN = 256
NSA_SEL_BLOCK = 64
NSA_TOPN = 16
NSA_WINDOW = 512
MOBA_HEADS = 4
MOBA_W = MOBA_HEADS * HEAD_DIM
MOBA_BLOCK = 256
MOBA_TOPK = 3
CONV_CH = 256
CONV_WIDTH = 31
HGRN_HEADS = 4
HGRN_KDIM = 64
HGRN_VDIM = 64
HGRN_W = HGRN_HEADS * HGRN_VDIM
HGRN_CHUNK = 64
MIX_WIDTH = NSA_W + MOBA_W + CONV_CH + HGRN_W
IN_WIDTHS = (NSA_W, HEAD_DIM, HEAD_DIM, HEAD_DIM, HEAD_DIM, HEAD_DIM, HEAD_DIM, 3 * NSA_HEADS,
             MOBA_W, MOBA_W, MOBA_W, 2 * CONV_CH, HGRN_W, HGRN_W, HGRN_W, HGRN_W)
IN_COLS = NSA_W + 6 * HEAD_DIM + 3 * NSA_HEADS + 3 * MOBA_W + 2 * CONV_CH + 4 * HGRN_W
D_FF = 4 * D_MODEL
REL_BUCKETS = 32
REL_MAX_EXACT = 16
REL_MAX_DIST = 128
N_ATTN_HEADS = NSA_HEADS + MOBA_HEADS
Q_BLOCK = 128
LN_EPS = 1e-5
RMS_EPS = 1e-6
BIG = 1e9
DEEPNORM_ALPHA = (2 * DEPTH) ** 0.25
DEEPNORM_BETA = (8 * DEPTH) ** -0.25

kernel_name = 'hymba_style_nsa_moba_conv_hgrn2_block'


def layer_norm(x, g, b):
    xf = x.astype(jnp.float32)
    mu = jnp.mean(xf, -1, keepdims=True)
    var = jnp.mean(jnp.square(xf - mu), -1, keepdims=True)
    return ((xf - mu) * lax.rsqrt(var + LN_EPS)).astype(x.dtype) * g + b


def masked_softmax(s, mask):
    s = jnp.where(mask, s.astype(jnp.float32), -jnp.inf)
    m = jnp.max(s, -1, keepdims=True)
    m = jnp.where(jnp.isfinite(m), m, 0.0)
    p = jnp.where(mask, jnp.exp(s - m), 0.0)
    return p / jnp.maximum(jnp.sum(p, -1, keepdims=True), 1e-30)


def rel_bucket(dist):
    n = jnp.maximum(dist, 0)
    nf = jnp.maximum(n, 1).astype(jnp.float32)
    large = REL_MAX_EXACT + (jnp.log(nf / REL_MAX_EXACT) / math.log(REL_MAX_DIST / REL_MAX_EXACT)
                             * (REL_BUCKETS - REL_MAX_EXACT)).astype(jnp.int32)
    return jnp.where(n < REL_MAX_EXACT, n, jnp.minimum(large, REL_BUCKETS - 1))


def nsa_compress(k, pe, w1, b1, w2):
    B, S, dh = k.shape
    n_cmp = (S - NSA_CMP_LEN) // NSA_CMP_STRIDE + 1
    idx = jnp.arange(n_cmp)[:, None] * NSA_CMP_STRIDE + jnp.arange(NSA_CMP_LEN)[None, :]
    blocks = (k[:, idx] + pe).reshape(B, n_cmp, NSA_CMP_LEN * dh)
    return jax.nn.gelu(blocks @ w1 + b1) @ w2


def nsa_attention(q, kc, vc, ks, vs, kw, vw, gates, tab):
    B, S, H, dh = q.shape
    n_cmp = kc.shape[1]
    n_sel = S // NSA_SEL_BLOCK
    topn = min(NSA_TOPN, n_sel)
    scale = dh ** -0.5
    cmp_start = jnp.arange(n_cmp) * NSA_CMP_STRIDE
    cmp_end = cmp_start + NSA_CMP_LEN - 1
    sel_start = jnp.arange(n_sel) * NSA_SEL_BLOCK
    overlap = jnp.clip(jnp.minimum(cmp_end[:, None] + 1, sel_start[None, :] + NSA_SEL_BLOCK)
                       - jnp.maximum(cmp_start[:, None], sel_start[None, :]), 0, None).astype(jnp.float32) / NSA_CMP_LEN
    kw_pad = jnp.pad(kw, ((0, 0), (NSA_WINDOW, 0), (0, 0)))
    vw_pad = jnp.pad(vw, ((0, 0), (NSA_WINDOW, 0), (0, 0)))
    bi = jnp.arange(B)[:, None, None]
    jsel = jnp.arange(n_sel)

    def block(c):
        s0 = c * Q_BLOCK
        qb = lax.dynamic_slice_in_dim(q, s0, Q_BLOCK, axis=1) * scale
        gb = lax.dynamic_slice_in_dim(gates, s0, Q_BLOCK, axis=1)
        t = s0 + jnp.arange(Q_BLOCK)
        dist_c = t[:, None] - cmp_end[None, :]
        s_c = jnp.einsum('bthd,bnd->bhtn', qb, kc) + tab[:, rel_bucket(dist_c)][None]
        p_c = masked_softmax(s_c, dist_c >= 0)
        o_c = jnp.einsum('bhtn,bnd->bthd', p_c.astype(vc.dtype), vc)
        imp = jnp.einsum('bhtn,nj->btj', p_c, overlap)
        blk_t = t // NSA_SEL_BLOCK
        forced = (jsel[None, :] == 0) | (jsel[None, :] == blk_t[:, None]) | (jsel[None, :] == blk_t[:, None] - 1)
        valid = jsel[None, :] <= blk_t[:, None]
        imp = jnp.where(forced, BIG, jnp.where(valid, imp, -BIG))
        _, sel = lax.top_k(imp, topn)
        pos = (sel[..., None] * NSA_SEL_BLOCK + jnp.arange(NSA_SEL_BLOCK)).reshape(B, Q_BLOCK, topn * NSA_SEL_BLOCK)
        k_sel = ks[bi, pos]
        v_sel = vs[bi, pos]
        dist_s = t[None, :, None] - pos
        s_s = jnp.einsum('bthd,btkd->bhtk', qb, k_sel) + jnp.moveaxis(tab[:, rel_bucket(dist_s)], 0, 1)
        p_s = masked_softmax(s_s, (dist_s >= 0)[:, None])
        o_s = jnp.einsum('bhtk,btkd->bthd', p_s.astype(v_sel.dtype), v_sel)
        kwb = lax.dynamic_slice_in_dim(kw_pad, s0, Q_BLOCK + NSA_WINDOW, axis=1)
        vwb = lax.dynamic_slice_in_dim(vw_pad, s0, Q_BLOCK + NSA_WINDOW, axis=1)
        kpos = s0 - NSA_WINDOW + jnp.arange(Q_BLOCK + NSA_WINDOW)
        dist_w = t[:, None] - kpos[None, :]
        mask_w = (dist_w >= 0) & (dist_w < NSA_WINDOW) & (kpos[None, :] >= 0)
        s_w = jnp.einsum('bthd,bkd->bhtk', qb, kwb) + tab[:, rel_bucket(dist_w)][None]
        p_w = masked_softmax(s_w, mask_w)
        o_w = jnp.einsum('bhtk,bkd->bthd', p_w.astype(vwb.dtype), vwb)
        return gb[..., 0:1] * o_c + gb[..., 1:2] * o_s + gb[..., 2:3] * o_w

    out = lax.map(block, jnp.arange(S // Q_BLOCK))
    return jnp.moveaxis(out, 0, 1).reshape(B, S, H * dh)


def moba_attention(q, k, v, tab):
    B, S, H, dh = q.shape
    s_pad = -(-S // MOBA_BLOCK) * MOBA_BLOCK
    nb = s_pad // MOBA_BLOCK
    ktop = min(MOBA_TOPK, nb)
    scale = dh ** -0.5
    padw = ((0, 0), (0, s_pad - S), (0, 0), (0, 0))
    kbh = jnp.pad(k, padw).reshape(B, nb, MOBA_BLOCK, H, dh).transpose(0, 3, 1, 2, 4)
    vbh = jnp.pad(v, padw).reshape(B, nb, MOBA_BLOCK, H, dh).transpose(0, 3, 1, 2, 4)
    k_mean = jnp.mean(kbh, axis=3)
    bi = jnp.arange(B)[:, None, None, None]
    hi = jnp.arange(H)[None, :, None, None]
    jblk = jnp.arange(nb)

    def block(c):
        s0 = c * Q_BLOCK
        qb = lax.dynamic_slice_in_dim(q, s0, Q_BLOCK, axis=1) * scale
        t = s0 + jnp.arange(Q_BLOCK)
        own = s0 // MOBA_BLOCK
        gate = jnp.einsum('bthd,bhnd->bhtn', qb, k_mean).astype(jnp.float32)
        gate = jnp.where((jblk < own)[None, None, None, :], gate, -BIG)
        gv, sel = lax.top_k(gate, ktop)
        sel_ok = gv > -0.5 * BIG
        k_sel = kbh[bi, hi, sel].reshape(B, H, Q_BLOCK, ktop * MOBA_BLOCK, dh)
        v_sel = vbh[bi, hi, sel].reshape(B, H, Q_BLOCK, ktop * MOBA_BLOCK, dh)
        pos_sel = (sel[..., None] * MOBA_BLOCK + jnp.arange(MOBA_BLOCK)).reshape(B, H, Q_BLOCK, ktop * MOBA_BLOCK)
        bias_sel = tab[hi, rel_bucket(t[None, None, :, None] - pos_sel)]
        s_sel = jnp.einsum('bthd,bhtkd->bhtk', qb, k_sel) + bias_sel
        mask_sel = jnp.repeat(sel_ok, MOBA_BLOCK, axis=-1)
        k_own = lax.dynamic_slice_in_dim(kbh, own, 1, axis=2)[:, :, 0]
        v_own = lax.dynamic_slice_in_dim(vbh, own, 1, axis=2)[:, :, 0]
        pos_own = own * MOBA_BLOCK + jnp.arange(MOBA_BLOCK)
        dist_own = t[:, None] - pos_own[None, :]
        s_own = jnp.einsum('bthd,bhkd->bhtk', qb, k_own) + tab[:, rel_bucket(dist_own)][None]
        mask_own = jnp.broadcast_to(dist_own >= 0, (B, H, Q_BLOCK, MOBA_BLOCK))
        p = masked_softmax(jnp.concatenate([s_sel, s_own], -1), jnp.concatenate([mask_sel, mask_own], -1))
        p_sel = p[..., :ktop * MOBA_BLOCK].astype(v.dtype)
        p_own = p[..., ktop * MOBA_BLOCK:].astype(v.dtype)
        return jnp.einsum('bhtk,bhtkd->bthd', p_sel, v_sel) + jnp.einsum('bhtk,bhkd->bthd', p_own, v_own)

    out = lax.map(block, jnp.arange(S // Q_BLOCK))
    return jnp.moveaxis(out, 0, 1).reshape(B, S, H * dh)


def conformer_conv(u, w_dw, b_dw, ln_g, ln_b, w_pw):
    a, g = jnp.split(u, 2, axis=-1)
    h = a * jax.nn.sigmoid(g)
    h = lax.conv_general_dilated(h, w_dw[:, None, :], window_strides=(1,),
                                 padding=[(CONV_WIDTH - 1, 0)],
                                 dimension_numbers=('NWC', 'WIO', 'NWC'),
                                 feature_group_count=CONV_CH) + b_dw
    h = jax.nn.silu(layer_norm(h, ln_g, ln_b))
    return h @ w_pw


def hgrn2_mixer(q, f_logit, i, g, lb, norm_g):
    B, S, _ = q.shape
    H, dk, dv, C = HGRN_HEADS, HGRN_KDIM, HGRN_VDIM, HGRN_CHUNK
    nc = S // C
    f = lb + (1.0 - lb) * jax.nn.sigmoid(f_logit.astype(jnp.float32))
    log_f = jnp.log(f)
    k = 1.0 - f

    def chunks(a, d):
        return a.astype(jnp.float32).reshape(B, nc, C, H, d).transpose(1, 0, 3, 2, 4)

    causal = jnp.tril(jnp.ones((C, C), dtype=bool))[:, :, None]

    def step(state, inp):
        qc, kc, vc, gc = inp
        cum = jnp.cumsum(gc, axis=2)
        diff = cum[:, :, :, None, :] - cum[:, :, None, :, :]
        decay = jnp.where(causal, jnp.exp(jnp.where(causal, diff, 0.0)), 0.0)
        attn = jnp.einsum('bhtk,bhsk,bhtsk->bhts', qc, kc, decay)
        o = jnp.einsum('bhtk,bhkv->bhtv', qc * jnp.exp(cum), state) + jnp.einsum('bhts,bhsv->bhtv', attn, vc)
        last = cum[:, :, -1:, :]
        state = jnp.exp(last[:, :, 0, :])[..., None] * state + jnp.einsum('bhsk,bhsv->bhkv', kc * jnp.exp(last - cum), vc)
        return state, o

    s_init = jnp.zeros((B, H, dk, dv), jnp.float32)
    _, o = lax.scan(step, s_init, (chunks(q, dk), chunks(k, dk), chunks(i, dv), chunks(log_f, dk)))
    o = o.transpose(1, 0, 3, 2, 4).reshape(B, S, H, dv)
    o = o * lax.rsqrt(jnp.mean(o * o, -1, keepdims=True) + RMS_EPS)
    return o.reshape(B, S, H * dv).astype(q.dtype) * norm_g * jax.nn.sigmoid(g)


def hybrid_layer(x, w_in, cmp_pe, cmp_w1, cmp_b1, cmp_w2, dw_w, dw_b, cln_g, cln_b, pw_w,
                 lb, hn_g, w_out, ln1_g, ln1_b, w_ff1, w_ff2, ln2_g, ln2_b, rel_tab):
    B, S, _ = x.shape
    split_points = [int(v) for v in np.cumsum(IN_WIDTHS)[:-1]]
    h = x @ w_in
    (q_n, kc, vc, ks, vs, kw, vw, g_n, q_m, k_m, v_m, conv_in,
     q_h, f_h, i_h, g_h) = jnp.split(h, split_points, axis=-1)
    kc = nsa_compress(kc, cmp_pe[0], cmp_w1[0], cmp_b1[0], cmp_w2[0])
    vc = nsa_compress(vc, cmp_pe[1], cmp_w1[1], cmp_b1[1], cmp_w2[1])
    gates = jax.nn.sigmoid(g_n.reshape(B, S, NSA_HEADS, 3))
    o_nsa = nsa_attention(q_n.reshape(B, S, NSA_HEADS, HEAD_DIM), kc, vc, ks, vs, kw, vw, gates,
                          rel_tab[:NSA_HEADS])
    o_moba = moba_attention(q_m.reshape(B, S, MOBA_HEADS, HEAD_DIM), k_m.reshape(B, S, MOBA_HEADS, HEAD_DIM),
                            v_m.reshape(B, S, MOBA_HEADS, HEAD_DIM), rel_tab[NSA_HEADS:])
    o_conv = conformer_conv(conv_in, dw_w, dw_b, cln_g, cln_b, pw_w)
    o_hgrn = hgrn2_mixer(q_h, f_h, i_h, g_h, lb, hn_g)
    mixed = jnp.concatenate([o_nsa, o_moba, o_conv, o_hgrn], axis=-1) @ w_out
    x = layer_norm(DEEPNORM_ALPHA * x + mixed, ln1_g, ln1_b)
    ff = jnp.square(jax.nn.relu(x @ w_ff1)) @ w_ff2
    return layer_norm(DEEPNORM_ALPHA * x + ff, ln2_g, ln2_b)


def setup_inputs(seed: int = 0) -> dict:
    key = jax.random.key(seed)
    ks = jax.random.split(key, 24)

    def nrm(k, shape, scale):
        return jax.random.normal(k, shape, jnp.float32) * scale

    return {
        'x': nrm(ks[0], (BATCH, SEQ, D_MODEL), 1.0),
        'w_in': nrm(ks[1], (DEPTH, D_MODEL, IN_COLS), D_MODEL ** -0.5),
        'nsa_cmp_pe': nrm(ks[2], (DEPTH, 2, NSA_CMP_LEN, HEAD_DIM), 0.1),
        'nsa_cmp_w1': nrm(ks[3], (DEPTH, 2, NSA_CMP_LEN * HEAD_DIM, NSA_CMP_HIDDEN), (NSA_CMP_LEN * HEAD_DIM) ** -0.5),
        'nsa_cmp_b1': nrm(ks[4], (DEPTH, 2, NSA_CMP_HIDDEN), 0.02),
        'nsa_cmp_w2': nrm(ks[5], (DEPTH, 2, NSA_CMP_HIDDEN, HEAD_DIM), NSA_CMP_HIDDEN ** -0.5),
        'conv_dw_w': nrm(ks[6], (DEPTH, CONV_WIDTH, CONV_CH), CONV_WIDTH ** -0.5),
        'conv_dw_b': nrm(ks[7], (DEPTH, CONV_CH), 0.02),
        'conv_ln_g': 1.0 + nrm(ks[8], (DEPTH, CONV_CH), 0.02),
        'conv_ln_b': nrm(ks[9], (DEPTH, CONV_CH), 0.02),
        'conv_pw_w': nrm(ks[10], (DEPTH, CONV_CH, CONV_CH), CONV_CH ** -0.5),
        'hgrn_lb_logits': nrm(ks[11], (DEPTH, HGRN_HEADS * HGRN_KDIM), 0.5),
        'hgrn_norm_g': 1.0 + nrm(ks[12], (DEPTH, HGRN_W), 0.02),
        'w_out': nrm(ks[13], (DEPTH, MIX_WIDTH, D_MODEL), MIX_WIDTH ** -0.5 * DEEPNORM_BETA),
        'ln1_g': 1.0 + nrm(ks[14], (DEPTH, D_MODEL), 0.02),
        'ln1_b': nrm(ks[15], (DEPTH, D_MODEL), 0.02),
        'w_ff1': nrm(ks[16], (DEPTH, D_MODEL, D_FF), D_MODEL ** -0.5),
        'w_ff2': nrm(ks[17], (DEPTH, D_FF, D_MODEL), D_FF ** -0.5 * DEEPNORM_BETA),
        'ln2_g': 1.0 + nrm(ks[18], (DEPTH, D_MODEL), 0.02),
        'ln2_b': nrm(ks[19], (DEPTH, D_MODEL), 0.02),
        'rel_bias': nrm(ks[20], (N_ATTN_HEADS, REL_BUCKETS), 0.5),
    }


def reference(x, w_in, nsa_cmp_pe, nsa_cmp_w1, nsa_cmp_b1, nsa_cmp_w2, conv_dw_w, conv_dw_b,
              conv_ln_g, conv_ln_b, conv_pw_w, hgrn_lb_logits, hgrn_norm_g, w_out, ln1_g, ln1_b,
              w_ff1, w_ff2, ln2_g, ln2_b, rel_bias):
    lb_sm = jax.nn.softmax(hgrn_lb_logits.astype(jnp.float32), axis=0)
    lbs = (jnp.cumsum(lb_sm, axis=0) - lb_sm[0]).astype(x.dtype)
    for l in range(DEPTH):
        x = hybrid_layer(x, w_in[l], nsa_cmp_pe[l], nsa_cmp_w1[l], nsa_cmp_b1[l], nsa_cmp_w2[l],
                         conv_dw_w[l], conv_dw_b[l], conv_ln_g[l], conv_ln_b[l], conv_pw_w[l],
                         lbs[l], hgrn_norm_g[l], w_out[l], ln1_g[l], ln1_b[l], w_ff1[l], w_ff2[l],
                         ln2_g[l], ln2_b[l], rel_bias)
    return x
```

```python
import functools
import math

import jax
import jax.numpy as jnp
import numpy as np
from jax import lax
from jax.experimental import pallas as pl
from jax.experimental.pallas import tpu as pltpu

D_MODEL = 1024
DEPTH = 2
HEAD_DIM = 64
NSA_HEADS = 4
NSA_W = NSA_HEADS * HEAD_DIM
NSA_CMP_LEN = 32
NSA_CMP_STRIDE = 16
NSA_CMP_HIDDEN = 256
NSA_SEL_BLOCK = 64
NSA_TOPN = 16
NSA_WINDOW = 512
MOBA_HEADS = 4
MOBA_W = MOBA_HEADS * HEAD_DIM
MOBA_BLOCK = 256
MOBA_TOPK = 3
CONV_CH = 256
CONV_WIDTH = 31
HGRN_HEADS = 4
HGRN_KDIM = 64
HGRN_VDIM = 64
HGRN_W = HGRN_HEADS * HGRN_VDIM
HGRN_CHUNK = 64
MIX_WIDTH = NSA_W + MOBA_W + CONV_CH + HGRN_W
IN_WIDTHS = (NSA_W, HEAD_DIM, HEAD_DIM, HEAD_DIM, HEAD_DIM, HEAD_DIM, HEAD_DIM, 3 * NSA_HEADS,
             MOBA_W, MOBA_W, MOBA_W, 2 * CONV_CH, HGRN_W, HGRN_W, HGRN_W, HGRN_W)
D_FF = 4 * D_MODEL
REL_BUCKETS = 32
REL_MAX_EXACT = 16
REL_MAX_DIST = 128
N_ATTN_HEADS = NSA_HEADS + MOBA_HEADS
Q_BLOCK = 128
LN_EPS = 1e-5
RMS_EPS = 1e-6
BIG = 1e9
DEEPNORM_ALPHA = (2 * DEPTH) ** 0.25

H_COLS = 3072
COL_QN = 0
COL_KVC = 256
COL_KVS = 384
COL_KVW = 512
COL_GN = 640
COL_QM = 768
COL_KM = 1024
COL_VM = 1280
COL_CONV = 1536
COL_QH = 2048
COL_FH = 2304
COL_IH = 2560
COL_GH = 2816

VMEM_LIMIT = 48 * 1024 * 1024
NEG = -1e30

_f32 = jnp.float32
_bf16 = jnp.bfloat16


def _cparams(sem):
    return pltpu.CompilerParams(dimension_semantics=sem, vmem_limit_bytes=VMEM_LIMIT)


def _layer_norm(z, g, b):
    mu = jnp.mean(z, axis=-1, keepdims=True)
    zc = z - mu
    var = jnp.mean(zc * zc, axis=-1, keepdims=True)
    return zc * lax.rsqrt(var + LN_EPS) * g + b


def _in_proj_kernel(x_ref, w_ref, o_ref):
    o_ref[...] = jnp.dot(x_ref[...].astype(_bf16), w_ref[...], preferred_element_type=_f32)


def in_proj(x2, w_bf16, tm=1024, tn=1024):
    n, d = x2.shape
    cols = w_bf16.shape[1]
    return pl.pallas_call(
        _in_proj_kernel,
        out_shape=jax.ShapeDtypeStruct((n, cols), _f32),
        grid=(n // tm, cols // tn),
        in_specs=[pl.BlockSpec((tm, d), lambda i, j: (i, 0)),
                  pl.BlockSpec((d, tn), lambda i, j: (0, j))],
        out_specs=pl.BlockSpec((tm, tn), lambda i, j: (i, j)),
        compiler_params=_cparams(("parallel", "arbitrary")),
        name="in_proj",
    )(x2, w_bf16)


def _regroup_w_in(w_in_l):
    sp = np.cumsum((0,) + IN_WIDTHS)
    seg = [w_in_l[:, sp[i]:sp[i + 1]] for i in range(len(IN_WIDTHS))]
    (q_n, kc, vc, ks, vs, kw, vw, g_n, q_m, k_m, v_m, conv_in, q_h, f_h, i_h, g_h) = seg
    pad = jnp.zeros((w_in_l.shape[0], 128 - 3 * NSA_HEADS), w_in_l.dtype)
    w = jnp.concatenate([q_n, kc, vc, ks, vs, kw, vw, g_n, pad, q_m, k_m, v_m, conv_in,
                         q_h, f_h, i_h, g_h], axis=1)
    assert w.shape[1] == H_COLS
    return w.astype(_bf16)


def _out_proj_kernel(a_ref, b_ref, c_ref, d_ref, x_ref, w_ref, g_ref, beta_ref, o_ref):
    mixed = jnp.dot(a_ref[...].astype(_bf16), w_ref[0:256, :], preferred_element_type=_f32)
    mixed += jnp.dot(b_ref[...].astype(_bf16), w_ref[256:512, :], preferred_element_type=_f32)
    mixed += jnp.dot(c_ref[...].astype(_bf16), w_ref[512:768, :], preferred_element_type=_f32)
    mixed += jnp.dot(d_ref[...].astype(_bf16), w_ref[768:1024, :], preferred_element_type=_f32)
    z = DEEPNORM_ALPHA * x_ref[...] + mixed
    o_ref[...] = _layer_norm(z, g_ref[...], beta_ref[...])


def out_proj_ln(o_nsa, o_moba, o_conv, o_hgrn, x2, w_out_bf16, g, b, tm=512):
    n, d = x2.shape
    part = pl.BlockSpec((tm, 256), lambda i: (i, 0))
    vec = pl.BlockSpec((1, d), lambda i: (0, 0))
    return pl.pallas_call(
        _out_proj_kernel,
        out_shape=jax.ShapeDtypeStruct((n, d), _f32),
        grid=(n // tm,),
        in_specs=[part, part, part, part,
                  pl.BlockSpec((tm, d), lambda i: (i, 0)),
                  pl.BlockSpec((MIX_WIDTH, d), lambda i: (0, 0)),
                  vec, vec],
        out_specs=pl.BlockSpec((tm, d), lambda i: (i, 0)),
        compiler_params=_cparams(("parallel",)),
        name="out_proj_ln",
    )(o_nsa, o_moba, o_conv, o_hgrn, x2, w_out_bf16, g.reshape(1, d), b.reshape(1, d))


def _mlp_kernel(x_ref, w1_ref, w2_ref, g_ref, beta_ref, o_ref, acc_ref):
    j = pl.program_id(1)

    @pl.when(j == 0)
    def _():
        acc_ref[...] = jnp.zeros_like(acc_ref)

    hid = jnp.dot(x_ref[...].astype(_bf16), w1_ref[...], preferred_element_type=_f32)
    hid = jnp.square(jnp.maximum(hid, 0.0))
    acc_ref[...] += jnp.dot(hid.astype(_bf16), w2_ref[...], preferred_element_type=_f32)

    @pl.when(j == pl.num_programs(1) - 1)
    def _():
        z = DEEPNORM_ALPHA * x_ref[...] + acc_ref[...]
        o_ref[...] = _layer_norm(z, g_ref[...], beta_ref[...])


def mlp_ln(x1, w1_bf16, w2_bf16, g, b, tm=1024, tf=1024):
    n, d = x1.shape
    dff = w1_bf16.shape[1]
    vec = pl.BlockSpec((1, d), lambda i, j: (0, 0))
    return pl.pallas_call(
        _mlp_kernel,
        out_shape=jax.ShapeDtypeStruct((n, d), _f32),
        grid=(n // tm, dff // tf),
        in_specs=[pl.BlockSpec((tm, d), lambda i, j: (i, 0)),
                  pl.BlockSpec((d, tf), lambda i, j: (0, j)),
                  pl.BlockSpec((tf, d), lambda i, j: (j, 0)),
                  vec, vec],
        out_specs=pl.BlockSpec((tm, d), lambda i, j: (i, 0)),
        scratch_shapes=[pltpu.VMEM((tm, d), _f32)],
        compiler_params=_cparams(("parallel", "arbitrary")),
        name="mlp_ln",
    )(x1, w1_bf16, w2_bf16, g.reshape(1, d), b.reshape(1, d))


CONV_HALO = 32


def _conv_kernel(u_ref, wdw_ref, bdw_ref, g_ref, beta_ref, wpw_ref, o_ref, buf_ref):
    t = pl.program_id(1)
    tt = u_ref.shape[0]

    @pl.when(t == 0)
    def _():
        buf_ref[0:CONV_HALO, :] = jnp.zeros((CONV_HALO, CONV_CH), _f32)

    @pl.when(t > 0)
    def _():
        buf_ref[0:CONV_HALO, :] = buf_ref[tt:tt + CONV_HALO, :]

    u = u_ref[...]
    a = u[:, :CONV_CH]
    gl = u[:, CONV_CH:]
    buf_ref[CONV_HALO:CONV_HALO + tt, :] = a * jax.nn.sigmoid(gl)

    acc = jnp.zeros((tt, CONV_CH), _f32) + bdw_ref[...]
    for w in range(CONV_WIDTH):
        off = CONV_HALO - (CONV_WIDTH - 1) + w
        acc = acc + buf_ref[off:off + tt, :] * wdw_ref[w:w + 1, :]
    hn = _layer_norm(acc, g_ref[...], beta_ref[...])
    hn = hn * jax.nn.sigmoid(hn)
    o_ref[...] = jnp.dot(hn.astype(_bf16), wpw_ref[...], preferred_element_type=_f32)


def conformer_conv(h, batch, seq, w_dw, b_dw, ln_g, ln_b, w_pw_bf16, tt=512):
    nt = seq // tt
    vec = pl.BlockSpec((1, CONV_CH), lambda b, t: (0, 0))
    return pl.pallas_call(
        _conv_kernel,
        out_shape=jax.ShapeDtypeStruct((batch * seq, CONV_CH), _f32),
        grid=(batch, nt),
        in_specs=[pl.BlockSpec((tt, 2 * CONV_CH), lambda b, t: (b * nt + t, COL_CONV // (2 * CONV_CH))),
                  pl.BlockSpec((CONV_WIDTH, CONV_CH), lambda b, t: (0, 0)),
                  vec, vec, vec,
                  pl.BlockSpec((CONV_CH, CONV_CH), lambda b, t: (0, 0))],
        out_specs=pl.BlockSpec((tt, CONV_CH), lambda b, t: (b * nt + t, 0)),
        scratch_shapes=[pltpu.VMEM((CONV_HALO + tt, CONV_CH), _f32)],
        compiler_params=_cparams(("parallel", "arbitrary")),
        name="conformer_conv",
    )(h, w_dw, b_dw.reshape(1, CONV_CH), ln_g.reshape(1, CONV_CH), ln_b.reshape(1, CONV_CH), w_pw_bf16)


HGRN_LEVELS = (32, 16, 8, 4, 2, 1)
HGRN_TILE = 256


def _hgrn_constants():
    c = HGRN_CHUNK
    t = np.arange(c)[:, None]
    u = np.arange(c)[None, :]
    blocks = [(u <= t), (u > t)]
    masks = []
    for h in HGRN_LEVELS:
        mid = (t // (2 * h)) * (2 * h) + h
        upper = t >= mid
        blocks.append(upper & (u >= mid) & (u <= t))
        blocks.append((~upper) & (u > t) & (u <= mid - 1))
        same = (t // (2 * h)) == (u // (2 * h))
        mid_s = (u // (2 * h)) * (2 * h) + h
        masks.append(same & upper & (u < mid_s))
    masks.append(t == u)
    lmat = np.concatenate(blocks, axis=0).astype(np.float32)
    mask = np.stack(masks).astype(np.float32)
    mask = np.tile(mask, (1, HGRN_HEADS, 1))
    lane_head = np.arange(HGRN_W)[None, :] // HGRN_KDIM
    lane_mask = (lane_head == np.arange(HGRN_HEADS)[:, None]).astype(np.float32)
    bd = (lane_head.T == lane_head).astype(np.float32)
    return lmat, mask, lane_mask, bd


def _split3(x):
    hi = x.astype(_bf16)
    r = x - hi.astype(_f32)
    mid = r.astype(_bf16)
    lo = (r - mid.astype(_f32)).astype(_bf16)
    return hi, mid, lo


def _dot(a, b):
    return jnp.dot(a, b, preferred_element_type=_f32)


def _dot_nt(a, b):
    return lax.dot_general(a, b, (((1,), (1,)), ((), ())), preferred_element_type=_f32)


def _dot_tn(a, b):
    return lax.dot_general(a, b, (((0,), (0,)), ((), ())), preferred_element_type=_f32)


def _exact_dot(l_bf16, x):
    hi, mid, lo = _split3(x)
    return _dot(l_bf16, hi) + _dot(l_bf16, mid) + _dot(l_bf16, lo)


def _hgrn_kernel(q_ref, f_ref, i_ref, g_ref, lb_ref, ng_ref, lmat_ref, mask_ref, lm_ref, bd_ref,
                 o_ref, st_ref):
    c = HGRN_CHUNK
    nl = len(HGRN_LEVELS)

    @pl.when(pl.program_id(1) == 0)
    def _():
        st_ref[...] = jnp.zeros_like(st_ref)

    lb = lb_ref[...]
    bd = bd_ref[...]

    def chunk(ci, carry):
        r0 = pl.multiple_of(ci * c, c)
        rows = pl.ds(r0, c)
        q = q_ref[rows, :]
        f = lb + (1.0 - lb) * jax.nn.sigmoid(f_ref[rows, :])
        logf = jnp.log(f)
        k = 1.0 - f
        v = i_ref[rows, :]
        v_b = v.astype(_bf16)
        ex = jnp.exp(_exact_dot(lmat_ref[...], logf))
        e_cum = ex[0:c]
        e_rest = ex[c:2 * c]

        attn = jnp.zeros((HGRN_HEADS * c, c), _f32)
        for l in range(nl + 1):
            if l < nl:
                ql = q * ex[(2 + 2 * l) * c:(3 + 2 * l) * c]
                kl = k * ex[(3 + 2 * l) * c:(4 + 2 * l) * c]
            else:
                ql, kl = q, k
            qs = jnp.concatenate([ql * lm_ref[h:h + 1, :] for h in range(HGRN_HEADS)], axis=0)
            attn = attn + _dot_nt(qs.astype(_bf16), kl.astype(_bf16)) * mask_ref[l]
        o = jnp.zeros((c, HGRN_W), _f32)
        for h in range(HGRN_HEADS):
            vh = (v * lm_ref[h:h + 1, :]).astype(_bf16)
            o = o + _dot(attn[h * c:(h + 1) * c, :].astype(_bf16), vh)

        st = st_ref[...]
        o = o + _dot_nt((q * e_cum).astype(_bf16), st.astype(_bf16))
        upd = _dot_tn(v_b, (k * e_rest).astype(_bf16)) * bd
        st_ref[...] = st * e_cum[c - 1:c, :] + upd

        osq = o * o
        hi, mid, lo = _split3(osq)
        bd_b = bd.astype(_bf16)
        ms = (_dot(hi, bd_b) + _dot(mid, bd_b) + _dot(lo, bd_b)) * (1.0 / HGRN_VDIM)
        o = o * lax.rsqrt(ms + RMS_EPS)
        o_ref[rows, :] = o * ng_ref[...] * jax.nn.sigmoid(g_ref[rows, :])
        return carry

    lax.fori_loop(0, q_ref.shape[0] // c, chunk, 0)


def hgrn2(h, batch, seq, lb, norm_g):
    lmat, mask, lane_mask, bd = _hgrn_constants()
    tt = HGRN_TILE
    nt = seq // tt

    def col(cb):
        return pl.BlockSpec((tt, HGRN_W), lambda b, t: (b * nt + t, cb))

    def const(shape):
        nd = len(shape)
        return pl.BlockSpec(shape, lambda b, t: (0,) * nd)

    return pl.pallas_call(
        _hgrn_kernel,
        out_shape=jax.ShapeDtypeStruct((batch * seq, HGRN_W), _f32),
        grid=(batch, nt),
        in_specs=[col(COL_QH // HGRN_W), col(COL_FH // HGRN_W), col(COL_IH // HGRN_W), col(COL_GH // HGRN_W),
                  const((1, HGRN_W)), const((1, HGRN_W)),
                  const(lmat.shape), const(mask.shape), const(lane_mask.shape), const(bd.shape)],
        out_specs=pl.BlockSpec((tt, HGRN_W), lambda b, t: (b * nt + t, 0)),
        scratch_shapes=[pltpu.VMEM((HGRN_W, HGRN_W), _f32)],
        compiler_params=_cparams(("parallel", "arbitrary")),
        name="hgrn2",
    )(h, h, h, h, lb.reshape(1, HGRN_W), norm_g.reshape(1, HGRN_W),
      jnp.asarray(lmat, _bf16), jnp.asarray(mask), jnp.asarray(lane_mask), jnp.asarray(bd))


def _rel_bucket_np(dist):
    n = np.maximum(dist, 0)
    nf = np.maximum(n, 1).astype(np.float32)
    large = REL_MAX_EXACT + (np.log(nf / np.float32(REL_MAX_EXACT)) / np.float32(math.log(REL_MAX_DIST / REL_MAX_EXACT))
                             * np.float32(REL_BUCKETS - REL_MAX_EXACT)).astype(np.int32)
    return np.where(n < REL_MAX_EXACT, n, np.minimum(large, REL_BUCKETS - 1)).astype(np.int32)


FAR_DIST = 113
assert int(_rel_bucket_np(np.arange(FAR_DIST, 4 * FAR_DIST)).min()) == REL_BUCKETS - 1


def _bias_table(tab, dist):
    dist = np.asarray(dist)
    vals = jnp.take(tab, jnp.asarray(_rel_bucket_np(dist).reshape(-1)), axis=1)
    vals = vals.reshape((tab.shape[0],) + dist.shape)
    return jnp.where(jnp.asarray(dist >= 0)[None], vals, NEG)


def _bias_table_rel(tab, dist):
    dist = np.asarray(dist)
    vals = _bias_table(tab, dist) - tab[:, REL_BUCKETS - 1].reshape((-1,) + (1,) * dist.ndim)
    return jnp.where(jnp.asarray(dist >= 0)[None], vals, NEG)


def _near_bias(tab):
    i = np.arange(Q_BLOCK)[:, None]
    j = np.arange(2 * Q_BLOCK)[None, :]
    return _bias_table_rel(tab, i - j + Q_BLOCK)


FAR_TILE = 512


def _softmax_update(state, s, msk, v_b):
    m, l, acc = state
    m_new = jnp.maximum(m, jnp.max(s, axis=1, keepdims=True))
    alpha = jnp.exp(m - m_new)
    p = jnp.where(msk, jnp.exp(s - m_new), 0.0)
    l = alpha * l + jnp.sum(p, axis=1, keepdims=True)
    acc = alpha * acc + _dot(p.astype(_bf16), v_b)
    return m_new, l, acc


def _moba_kernel(q_ref, k_ref, v_ref, bn_ref, o_ref, kmean_ref):
    c = pl.program_id(2)
    seq = k_ref.shape[0]
    nb = seq // MOBA_BLOCK
    tq = Q_BLOCK

    @pl.when(c == 0)
    def _():
        kmean_ref[...] = jnp.sum(k_ref[...].reshape(nb, MOBA_BLOCK, 128), axis=1) * (1.0 / MOBA_BLOCK)

    s0 = c * tq
    own = s0 // MOBA_BLOCK
    far_end = s0 - tq
    n_far = (jnp.maximum(far_end, 0) + FAR_TILE - 1) // FAR_TILE

    lane = lax.broadcasted_iota(jnp.int32, (1, 128), 1)
    jcol = lax.broadcasted_iota(jnp.int32, (nb, tq), 0)
    jrow_t = lax.broadcasted_iota(jnp.int32, (nb, FAR_TILE), 0)
    kpos_t = lax.broadcasted_iota(jnp.int32, (nb, FAR_TILE), 1)
    jrow_n = lax.broadcasted_iota(jnp.int32, (nb, 2 * tq), 0)
    kpos_n = lax.broadcasted_iota(jnp.int32, (nb, 2 * tq), 1) + far_end

    q_all = q_ref[...] * (HEAD_DIM ** -0.5)
    kmean_b = kmean_ref[...].astype(_bf16)
    qs, sels = [], []
    for hh in range(2):
        qh = jnp.where(lane // HEAD_DIM == hh, q_all, 0.0).astype(_bf16)
        gate = _dot_nt(kmean_b, qh)
        work = jnp.where(jcol < own, gate, -BIG)
        sel = jnp.zeros((nb, tq), _f32)
        for _ in range(MOBA_TOPK):
            mx = jnp.max(work, axis=0, keepdims=True)
            idx = jnp.min(jnp.where(work == mx, jcol, nb), axis=0, keepdims=True)
            pick = jcol == idx
            sel = jnp.where(pick & (mx > -0.5 * BIG), 1.0, sel)
            work = jnp.where(pick, -jnp.inf, work)
        sel = jnp.where(jcol == own, 1.0, sel)
        qs.append(qh)
        sels.append(sel.astype(_bf16))

    def far_body(kt, carry):
        base = pl.multiple_of(kt * FAR_TILE, FAR_TILE)
        k_b = k_ref[pl.ds(base, FAR_TILE), :].astype(_bf16)
        v_b = v_ref[pl.ds(base, FAR_TILE), :].astype(_bf16)
        expand = ((jrow_t == (base + kpos_t) // MOBA_BLOCK) & (base + kpos_t < far_end)).astype(_bf16)
        out = []
        for hh in range(2):
            msk = _dot_tn(sels[hh], expand) > 0.5
            s = jnp.where(msk, _dot_nt(qs[hh], k_b), NEG)
            out.append(_softmax_update(carry[hh], s, msk, v_b))
        return tuple(out)

    init = tuple((jnp.full((tq, 1), NEG, _f32), jnp.zeros((tq, 1), _f32), jnp.zeros((tq, 128), _f32))
                 for _ in range(2))
    state = lax.fori_loop(0, n_far, far_body, init)

    prev0 = pl.multiple_of(jnp.maximum(far_end, 0), tq)
    k_b = jnp.concatenate([k_ref[pl.ds(prev0, tq), :], k_ref[pl.ds(pl.multiple_of(s0, tq), tq), :]],
                          axis=0).astype(_bf16)
    v_b = jnp.concatenate([v_ref[pl.ds(prev0, tq), :], v_ref[pl.ds(pl.multiple_of(s0, tq), tq), :]],
                          axis=0).astype(_bf16)
    expand = ((jrow_n == kpos_n // MOBA_BLOCK) & (kpos_n >= 0)).astype(_bf16)
    outs = []
    for hh in range(2):
        bias = bn_ref[hh]
        msk = (_dot_tn(sels[hh], expand) > 0.5) & (bias > 0.5 * NEG)
        s = jnp.where(msk, _dot_nt(qs[hh], k_b) + bias, NEG)
        _, l, acc = _softmax_update(state[hh], s, msk, v_b)
        outs.append(acc / l)
    o_ref[...] = jnp.where(lane // HEAD_DIM == 0, outs[0], outs[1])


def moba_attention(h, batch, seq, tab):
    nq = seq // Q_BLOCK
    bias_near = _near_bias(tab)

    def kv(col):
        return pl.BlockSpec((seq, 128), lambda b, hp, c: (b, col // 128 + hp))

    return pl.pallas_call(
        _moba_kernel,
        out_shape=jax.ShapeDtypeStruct((batch * seq, MOBA_W), _f32),
        grid=(batch, MOBA_HEADS // 2, nq),
        in_specs=[pl.BlockSpec((Q_BLOCK, 128), lambda b, hp, c: (b * nq + c, COL_QM // 128 + hp)),
                  kv(COL_KM), kv(COL_VM),
                  pl.BlockSpec((2, Q_BLOCK, 2 * Q_BLOCK), lambda b, hp, c: (hp, 0, 0))],
        out_specs=pl.BlockSpec((Q_BLOCK, 128), lambda b, hp, c: (b * nq + c, hp)),
        scratch_shapes=[pltpu.VMEM((seq // MOBA_BLOCK, 128), _f32)],
        compiler_params=_cparams(("parallel", "parallel", "arbitrary")),
        name="moba_attention",
    )(h, h, h, bias_near)


def _compress_kernel(a_ref, pet_ref, peb_ref, wt_ref, wb_ref, b1_ref, w2_ref, o_ref):
    a = a_ref[...]
    nrow = a.shape[0]
    top = _dot((a + pet_ref[...]).astype(_bf16), wt_ref[...])
    bot = _dot((a + peb_ref[...]).astype(_bf16), wb_ref[...])
    pre = top + pltpu.roll(bot, nrow - 1, 0) + b1_ref[...]
    hid = jax.nn.gelu(pre)
    o_ref[...] = _dot(hid.astype(_bf16), w2_ref[...])


def nsa_compress(h, batch, seq, pe, w1, b1, w2):
    nr = seq // NSA_CMP_STRIDE
    half = NSA_CMP_LEN // 2
    a = h[:, COL_KVC:COL_KVC + 128].reshape(batch, nr, NSA_CMP_STRIDE * 128)
    pe_cat = jnp.concatenate([pe[0], pe[1]], axis=-1)
    pe_top = pe_cat[:half].reshape(1, half * 128)
    pe_bot = pe_cat[half:].reshape(1, half * 128)
    hid = NSA_CMP_HIDDEN
    w_all = jnp.zeros((NSA_CMP_LEN, 128, 2 * hid), _f32)
    w_all = w_all.at[:, :HEAD_DIM, :hid].set(w1[0].reshape(NSA_CMP_LEN, HEAD_DIM, hid))
    w_all = w_all.at[:, HEAD_DIM:, hid:].set(w1[1].reshape(NSA_CMP_LEN, HEAD_DIM, hid))
    w_top = w_all[:half].reshape(half * 128, 2 * hid).astype(_bf16)
    w_bot = w_all[half:].reshape(half * 128, 2 * hid).astype(_bf16)
    b1cat = jnp.concatenate([b1[0], b1[1]]).reshape(1, 2 * hid)
    w2bd = jnp.zeros((2 * hid, 128), _f32)
    w2bd = w2bd.at[:hid, :HEAD_DIM].set(w2[0]).at[hid:, HEAD_DIM:].set(w2[1]).astype(_bf16)

    def const(shape):
        return pl.BlockSpec(shape, lambda b: (0,) * len(shape))

    return pl.pallas_call(
        _compress_kernel,
        out_shape=jax.ShapeDtypeStruct((batch, nr, 128), _f32),
        grid=(batch,),
        in_specs=[pl.BlockSpec((None, nr, half * 128), lambda b: (b, 0, 0)),
                  const(pe_top.shape), const(pe_bot.shape), const(w_top.shape), const(w_bot.shape),
                  const(b1cat.shape), const(w2bd.shape)],
        out_specs=pl.BlockSpec((None, nr, 128), lambda b: (b, 0, 0)),
        compiler_params=_cparams(("parallel",)),
        name="nsa_compress",
    )(a, pe_top, pe_bot, w_top, w_bot, b1cat, w2bd)


CMP_NEAR = 16
NSA_WTILES = NSA_WINDOW // Q_BLOCK + 1


def _nsa_tables(tab, seq):
    nc = seq // NSA_CMP_STRIDE
    nsel = seq // NSA_SEL_BLOCK
    n_cmp = (seq - NSA_CMP_LEN) // NSA_CMP_STRIDE + 1
    i = np.arange(Q_BLOCK)[:, None]
    m = np.arange(CMP_NEAR)[None, :]
    d_c = i + (Q_BLOCK - NSA_CMP_LEN + 1) - NSA_CMP_STRIDE * m
    cmp_bias = _bias_table_rel(tab, np.maximum(d_c, 0))
    j = np.arange(NSA_WTILES * Q_BLOCK)[None, :]
    d_w = i - j + NSA_WINDOW
    win_bias = _bias_table_rel(tab, np.where(d_w < NSA_WINDOW, d_w, -1))
    n = np.arange(nc)[:, None]
    js = np.arange(nsel)[None, :]
    cs, ce = n * NSA_CMP_STRIDE, n * NSA_CMP_STRIDE + NSA_CMP_LEN
    ss = js * NSA_SEL_BLOCK
    ov = np.clip(np.minimum(ce, ss + NSA_SEL_BLOCK) - np.maximum(cs, ss), 0, None).astype(np.float32) / NSA_CMP_LEN
    ov = np.where(n < n_cmp, ov, 0.0)
    return cmp_bias, win_bias, _near_bias(tab), jnp.asarray(ov.T, _bf16)


def _exact_dot_r(x, r_bf16):
    hi, mid, lo = _split3(x)
    return _dot(hi, r_bf16) + _dot(mid, r_bf16) + _dot(lo, r_bf16)


def _nsa_kernel(q_ref, gn_ref, kvc_ref, kvs_ref, kvw_ref, cb_ref, wb_ref, nb_ref, ovt_ref, o_ref):
    c = pl.program_id(1)
    tq = Q_BLOCK
    nh = NSA_HEADS
    nc = kvc_ref.shape[0]
    nsel = ovt_ref.shape[0]
    s0 = c * tq
    far_end = s0 - tq
    lane = lax.broadcasted_iota(jnp.int32, (1, 128), 1)
    low = lane < HEAD_DIM

    q_all = q_ref[...] * (HEAD_DIM ** -0.5)
    qh = []
    for h in range(nh):
        blk = q_all[:, 128 * (h // 2):128 * (h // 2) + 128]
        if h % 2 == 1:
            blk = pltpu.roll(blk, HEAD_DIM, 1)
        qh.append(jnp.where(low, blk, 0.0))
    q_b = jnp.concatenate(qh, axis=0).astype(_bf16)

    kvc_b = kvc_ref[...].astype(_bf16)
    s_c = _dot_nt(q_b, kvc_b).reshape(nh, tq, nc)
    n_idx = lax.broadcasted_iota(jnp.int32, (tq, nc), 1)
    t_idx = lax.broadcasted_iota(jnp.int32, (tq, nc), 0) + s0
    valid_c = (n_idx * NSA_CMP_STRIDE + (NSA_CMP_LEN - 1)) <= t_idx
    m_sel = lax.broadcasted_iota(jnp.int32, (CMP_NEAR, nc), 0)
    n_sel = lax.broadcasted_iota(jnp.int32, (CMP_NEAR, nc), 1)
    place = (n_sel == (s0 // NSA_CMP_STRIDE) - CMP_NEAR // 2 + m_sel).astype(_bf16)
    psum = jnp.zeros((tq, nc), _f32)
    o_c = []
    for h in range(nh):
        s = s_c[h] + _exact_dot_r(cb_ref[h], place)
        s = jnp.where(valid_c, s, NEG)
        mx = jnp.max(s, axis=1, keepdims=True)
        e = jnp.where(valid_c, jnp.exp(s - mx), 0.0)
        p = e / jnp.maximum(jnp.sum(e, axis=1, keepdims=True), 1e-30)
        psum = psum + p
        o_c.append(_dot(p.astype(_bf16), kvc_b))

    imp_t = jnp.zeros((nsel, tq), _f32)
    for part in _split3(psum):
        imp_t = imp_t + _dot_nt(ovt_ref[...], part)
    jblk = lax.broadcasted_iota(jnp.int32, (nsel, tq), 0)
    blk_t = (lax.broadcasted_iota(jnp.int32, (nsel, tq), 1) + s0) // NSA_SEL_BLOCK
    forced = (jblk == 0) | (jblk == blk_t) | (jblk == blk_t - 1)
    work = jnp.where(forced, BIG, jnp.where(jblk <= blk_t, imp_t, -BIG))
    sel = jnp.zeros((nsel, tq), _f32)
    for _ in range(min(NSA_TOPN, nsel)):
        mx = jnp.max(work, axis=0, keepdims=True)
        idx = jnp.min(jnp.where(work == mx, jblk, nsel), axis=0, keepdims=True)
        pick = jblk == idx
        sel = jnp.where(pick, 1.0, sel)
        work = jnp.where(pick, -jnp.inf, work)
    sel_b = sel.astype(_bf16)

    jrow_t = lax.broadcasted_iota(jnp.int32, (nsel, FAR_TILE), 0)
    kpos_t = lax.broadcasted_iota(jnp.int32, (nsel, FAR_TILE), 1)

    def far_body(kt, carry):
        m, l, acc = carry
        base = pl.multiple_of(kt * FAR_TILE, FAR_TILE)
        kv_b = kvs_ref[pl.ds(base, FAR_TILE), :].astype(_bf16)
        expand = ((jrow_t == (base + kpos_t) // NSA_SEL_BLOCK) & (base + kpos_t < far_end)).astype(_bf16)
        msk = (_dot_tn(sel_b, expand) > 0.5)[None]
        s = jnp.where(msk, _dot_nt(q_b, kv_b).reshape(nh, tq, FAR_TILE), NEG)
        m_new = jnp.maximum(m, jnp.max(s, axis=2, keepdims=True))
        alpha = jnp.exp(m - m_new)
        p = jnp.where(msk, jnp.exp(s - m_new), 0.0)
        l = alpha * l + jnp.sum(p, axis=2, keepdims=True)
        pv = _dot(p.reshape(nh * tq, FAR_TILE).astype(_bf16), kv_b).reshape(nh, tq, 128)
        return m_new, l, alpha * acc + pv

    n_far = (jnp.maximum(far_end, 0) + FAR_TILE - 1) // FAR_TILE
    init = (jnp.full((nh, tq, 1), NEG, _f32), jnp.zeros((nh, tq, 1), _f32), jnp.zeros((nh, tq, 128), _f32))
    m, l, acc = lax.fori_loop(0, n_far, far_body, init)

    prev0 = pl.multiple_of(jnp.maximum(far_end, 0), tq)
    own0 = pl.multiple_of(s0, tq)
    kv_b = jnp.concatenate([kvs_ref[pl.ds(prev0, tq), :], kvs_ref[pl.ds(own0, tq), :]], axis=0).astype(_bf16)
    jrow_n = lax.broadcasted_iota(jnp.int32, (nsel, 2 * tq), 0)
    kpos_n = lax.broadcasted_iota(jnp.int32, (nsel, 2 * tq), 1) + far_end
    expand = ((jrow_n == kpos_n // NSA_SEL_BLOCK) & (kpos_n >= 0)).astype(_bf16)
    nbias = nb_ref[...]
    msk = (_dot_tn(sel_b, expand) > 0.5)[None] & (nbias > 0.5 * NEG)
    s = jnp.where(msk, _dot_nt(q_b, kv_b).reshape(nh, tq, 2 * tq) + nbias, NEG)
    m_new = jnp.maximum(m, jnp.max(s, axis=2, keepdims=True))
    alpha = jnp.exp(m - m_new)
    p = jnp.where(msk, jnp.exp(s - m_new), 0.0)
    l = alpha * l + jnp.sum(p, axis=2, keepdims=True)
    pv = _dot(p.reshape(nh * tq, 2 * tq).astype(_bf16), kv_b).reshape(nh, tq, 128)
    o_s = (alpha * acc + pv) / l

    tiles = []
    for r in range(NSA_WTILES):
        start = pl.multiple_of(jnp.maximum(s0 - NSA_WINDOW + r * tq, 0), tq)
        tiles.append(kvw_ref[pl.ds(start, tq), :])
    kvw_b = jnp.concatenate(tiles, axis=0).astype(_bf16)
    wk = NSA_WTILES * tq
    kpos_w = lax.broadcasted_iota(jnp.int32, (1, 1, wk), 2) + (s0 - NSA_WINDOW)
    s = _dot_nt(q_b, kvw_b).reshape(nh, tq, wk) + wb_ref[...]
    s = jnp.where(kpos_w >= 0, s, NEG)
    mx = jnp.max(s, axis=2, keepdims=True)
    e = jnp.exp(s - mx)
    lw = jnp.sum(e, axis=2, keepdims=True)
    o_w = _dot(e.reshape(nh * tq, wk).astype(_bf16), kvw_b).reshape(nh, tq, 128) / lw

    g = jax.nn.sigmoid(gn_ref[...])
    outs = []
    for h in range(nh):
        outs.append(g[:, 3 * h:3 * h + 1] * o_c[h] + g[:, 3 * h + 1:3 * h + 2] * o_s[h]
                    + g[:, 3 * h + 2:3 * h + 3] * o_w[h])
    for hp in range(nh // 2):
        o_ref[:, 128 * hp:128 * hp + 128] = jnp.where(low, pltpu.roll(outs[2 * hp], HEAD_DIM, 1), outs[2 * hp + 1])


def nsa_attention(h, kvc, batch, seq, tab):
    nq = seq // Q_BLOCK
    nc = seq // NSA_CMP_STRIDE
    cmp_bias, win_bias, near_bias, ov_t = _nsa_tables(tab, seq)

    def const(shape):
        return pl.BlockSpec(shape, lambda b, c: (0,) * len(shape))

    return pl.pallas_call(
        _nsa_kernel,
        out_shape=jax.ShapeDtypeStruct((batch * seq, NSA_W), _f32),
        grid=(batch, nq),
        in_specs=[pl.BlockSpec((Q_BLOCK, NSA_W), lambda b, c: (b * nq + c, COL_QN // NSA_W)),
                  pl.BlockSpec((Q_BLOCK, 128), lambda b, c: (b * nq + c, COL_GN // 128)),
                  pl.BlockSpec((None, nc, 128), lambda b, c: (b, 0, 0)),
                  pl.BlockSpec((seq, 128), lambda b, c: (b, COL_KVS // 128)),
                  pl.BlockSpec((seq, 128), lambda b, c: (b, COL_KVW // 128)),
                  const(cmp_bias.shape), const(win_bias.shape), const(near_bias.shape), const(ov_t.shape)],
        out_specs=pl.BlockSpec((Q_BLOCK, NSA_W), lambda b, c: (b * nq + c, 0)),
        compiler_params=_cparams(("parallel", "arbitrary")),
        name="nsa_attention",
    )(h, h, kvc, h, h, cmp_bias, win_bias, near_bias, ov_t)


def kernel(x, w_in, nsa_cmp_pe, nsa_cmp_w1, nsa_cmp_b1, nsa_cmp_w2, conv_dw_w, conv_dw_b, conv_ln_g,
           conv_ln_b, conv_pw_w, hgrn_lb_logits, hgrn_norm_g, w_out, ln1_g, ln1_b, w_ff1, w_ff2,
           ln2_g, ln2_b, rel_bias):
    batch, seq, d = x.shape
    lb_sm = jax.nn.softmax(hgrn_lb_logits.astype(_f32), axis=0)
    lbs = jnp.cumsum(lb_sm, axis=0) - lb_sm[0]
    x2 = x.reshape(batch * seq, d)
    for l in range(DEPTH):
        h = in_proj(x2, _regroup_w_in(w_in[l]))
        o_conv = conformer_conv(h, batch, seq, conv_dw_w[l], conv_dw_b[l], conv_ln_g[l], conv_ln_b[l],
                                conv_pw_w[l].astype(_bf16))
        o_hgrn = hgrn2(h, batch, seq, lbs[l], hgrn_norm_g[l])
        kvc = nsa_compress(h, batch, seq, nsa_cmp_pe[l], nsa_cmp_w1[l], nsa_cmp_b1[l], nsa_cmp_w2[l])
        o_nsa = nsa_attention(h, kvc, batch, seq, rel_bias[:NSA_HEADS])
        o_moba = moba_attention(h, batch, seq, rel_bias[NSA_HEADS:])
        x1 = out_proj_ln(o_nsa, o_moba, o_conv, o_hgrn, x2, w_out[l].astype(_bf16), ln1_g[l], ln1_b[l])
        x2 = mlp_ln(x1, w_ff1[l].astype(_bf16), w_ff2[l].astype(_bf16), ln2_g[l], ln2_b[l])
    return x2.reshape(batch, seq, d)
```

```python
import math

import jax
import jax.numpy as jnp
import numpy as np
from jax import lax
from jax.experimental import pallas as pl
from jax.experimental.pallas import tpu as pltpu

D_MODEL = 1024
DEPTH = 2
HEAD_DIM = 64
NSA_HEADS = 4
NSA_W = NSA_HEADS * HEAD_DIM
NSA_CMP_LEN = 32
NSA_CMP_STRIDE = 16
NSA_CMP_HIDDEN = 256
NSA_SEL_BLOCK = 64
NSA_TOPN = 16
NSA_WINDOW = 512
MOBA_HEADS = 4
MOBA_W = MOBA_HEADS * HEAD_DIM
MOBA_BLOCK = 256
MOBA_TOPK = 3
CONV_CH = 256
CONV_WIDTH = 31
HGRN_HEADS = 4
HGRN_KDIM = 64
HGRN_VDIM = 64
HGRN_W = HGRN_HEADS * HGRN_VDIM
HGRN_CHUNK = 64
MIX_WIDTH = NSA_W + MOBA_W + CONV_CH + HGRN_W
IN_WIDTHS = (NSA_W, HEAD_DIM, HEAD_DIM, HEAD_DIM, HEAD_DIM, HEAD_DIM, HEAD_DIM, 3 * NSA_HEADS,
             MOBA_W, MOBA_W, MOBA_W, 2 * CONV_CH, HGRN_W, HGRN_W, HGRN_W, HGRN_W)
D_FF = 4 * D_MODEL
REL_BUCKETS = 32
REL_MAX_EXACT = 16
REL_MAX_DIST = 128
TQ = 256
PREV = 128
LN_EPS = 1e-5
RMS_EPS = 1e-6
BIG = 1e9
DEEPNORM_ALPHA = (2 * DEPTH) ** 0.25

H_COLS = 3072
COL_QN = 0
COL_KVC = 256
COL_KVS = 384
COL_KVW = 512
COL_GN = 640
COL_QM = 768
COL_KM = 1024
COL_VM = 1280
COL_CONV = 1536
COL_QH = 2048
COL_FH = 2304
COL_IH = 2560
COL_GH = 2816

VMEM_LIMIT = 48 * 1024 * 1024
NEG = -1e30

_f32 = jnp.float32
_bf16 = jnp.bfloat16


def _cparams(sem):
    return pltpu.CompilerParams(dimension_semantics=sem, vmem_limit_bytes=VMEM_LIMIT)


def _layer_norm(z, g, b):
    mu = jnp.mean(z, axis=-1, keepdims=True)
    zc = z - mu
    var = jnp.mean(zc * zc, axis=-1, keepdims=True)
    return zc * lax.rsqrt(var + LN_EPS) * g + b


def _in_proj_kernel(x_ref, w_ref, o_ref, ob_ref):
    acc = jnp.dot(x_ref[...].astype(_bf16), w_ref[...], preferred_element_type=_f32)
    o_ref[...] = acc
    ob_ref[...] = acc.astype(_bf16)


def in_proj(x2, w_bf16, tm=1024, tn=1024):
    n, d = x2.shape
    cols = w_bf16.shape[1]
    out = pl.BlockSpec((tm, tn), lambda i, j: (i, j))
    return pl.pallas_call(
        _in_proj_kernel,
        out_shape=(jax.ShapeDtypeStruct((n, cols), _f32), jax.ShapeDtypeStruct((n, cols), _bf16)),
        grid=(n // tm, cols // tn),
        in_specs=[pl.BlockSpec((tm, d), lambda i, j: (i, 0)),
                  pl.BlockSpec((d, tn), lambda i, j: (0, j))],
        out_specs=(out, out),
        compiler_params=_cparams(("parallel", "arbitrary")),
        name="in_proj",
    )(x2, w_bf16)


def _regroup_w_in(w_in_l):
    sp = np.cumsum((0,) + IN_WIDTHS)
    seg = [w_in_l[:, sp[i]:sp[i + 1]] for i in range(len(IN_WIDTHS))]
    (q_n, kc, vc, ks, vs, kw, vw, g_n, q_m, k_m, v_m, conv_in, q_h, f_h, i_h, g_h) = seg
    pad = jnp.zeros((w_in_l.shape[0], 128 - 3 * NSA_HEADS), w_in_l.dtype)
    w = jnp.concatenate([q_n, kc, vc, ks, vs, kw, vw, g_n, pad, q_m, k_m, v_m, conv_in,
                         q_h, f_h, i_h, g_h], axis=1)
    assert w.shape[1] == H_COLS
    return w.astype(_bf16)


def _out_proj_kernel(a_ref, b_ref, c_ref, d_ref, x_ref, w_ref, g_ref, beta_ref, o_ref):
    mixed = jnp.dot(a_ref[...].astype(_bf16), w_ref[0:256, :], preferred_element_type=_f32)
    mixed += jnp.dot(b_ref[...].astype(_bf16), w_ref[256:512, :], preferred_element_type=_f32)
    mixed += jnp.dot(c_ref[...].astype(_bf16), w_ref[512:768, :], preferred_element_type=_f32)
    mixed += jnp.dot(d_ref[...].astype(_bf16), w_ref[768:1024, :], preferred_element_type=_f32)
    z = DEEPNORM_ALPHA * x_ref[...] + mixed
    o_ref[...] = _layer_norm(z, g_ref[...], beta_ref[...])


def out_proj_ln(o_nsa, o_moba, o_conv, o_hgrn, x2, w_out_bf16, g, b, tm=512):
    n, d = x2.shape
    part = pl.BlockSpec((tm, 256), lambda i: (i, 0))
    vec = pl.BlockSpec((1, d), lambda i: (0, 0))
    return pl.pallas_call(
        _out_proj_kernel,
        out_shape=jax.ShapeDtypeStruct((n, d), _f32),
        grid=(n // tm,),
        in_specs=[part, part, part, part,
                  pl.BlockSpec((tm, d), lambda i: (i, 0)),
                  pl.BlockSpec((MIX_WIDTH, d), lambda i: (0, 0)),
                  vec, vec],
        out_specs=pl.BlockSpec((tm, d), lambda i: (i, 0)),
        compiler_params=_cparams(("parallel",)),
        name="out_proj_ln",
    )(o_nsa, o_moba, o_conv, o_hgrn, x2, w_out_bf16, g.reshape(1, d), b.reshape(1, d))


def _mlp_kernel(x_ref, w1_ref, w2_ref, g_ref, beta_ref, o_ref, acc_ref):
    j = pl.program_id(1)

    @pl.when(j == 0)
    def _():
        acc_ref[...] = jnp.zeros_like(acc_ref)

    hid = jnp.dot(x_ref[...].astype(_bf16), w1_ref[...], preferred_element_type=_f32)
    hid = jnp.square(jnp.maximum(hid, 0.0))
    acc_ref[...] += jnp.dot(hid.astype(_bf16), w2_ref[...], preferred_element_type=_f32)

    @pl.when(j == pl.num_programs(1) - 1)
    def _():
        z = DEEPNORM_ALPHA * x_ref[...] + acc_ref[...]
        o_ref[...] = _layer_norm(z, g_ref[...], beta_ref[...])


def mlp_ln(x1, w1_bf16, w2_bf16, g, b, tm=1024, tf=1024):
    n, d = x1.shape
    dff = w1_bf16.shape[1]
    vec = pl.BlockSpec((1, d), lambda i, j: (0, 0))
    return pl.pallas_call(
        _mlp_kernel,
        out_shape=jax.ShapeDtypeStruct((n, d), _f32),
        grid=(n // tm, dff // tf),
        in_specs=[pl.BlockSpec((tm, d), lambda i, j: (i, 0)),
                  pl.BlockSpec((d, tf), lambda i, j: (0, j)),
                  pl.BlockSpec((tf, d), lambda i, j: (j, 0)),
                  vec, vec],
        out_specs=pl.BlockSpec((tm, d), lambda i, j: (i, 0)),
        scratch_shapes=[pltpu.VMEM((tm, d), _f32)],
        compiler_params=_cparams(("parallel", "arbitrary")),
        name="mlp_ln",
    )(x1, w1_bf16, w2_bf16, g.reshape(1, d), b.reshape(1, d))


CONV_HALO = 32


def _conv_kernel(u_ref, wdw_ref, bdw_ref, g_ref, beta_ref, wpw_ref, o_ref, buf_ref):
    t = pl.program_id(1)
    tt = u_ref.shape[0]

    @pl.when(t == 0)
    def _():
        buf_ref[0:CONV_HALO, :] = jnp.zeros((CONV_HALO, CONV_CH), _f32)

    @pl.when(t > 0)
    def _():
        buf_ref[0:CONV_HALO, :] = buf_ref[tt:tt + CONV_HALO, :]

    u = u_ref[...]
    a = u[:, :CONV_CH]
    gl = u[:, CONV_CH:]
    buf_ref[CONV_HALO:CONV_HALO + tt, :] = a * jax.nn.sigmoid(gl)

    acc = jnp.zeros((tt, CONV_CH), _f32) + bdw_ref[...]
    for w in range(CONV_WIDTH):
        off = CONV_HALO - (CONV_WIDTH - 1) + w
        acc = acc + buf_ref[off:off + tt, :] * wdw_ref[w:w + 1, :]
    hn = _layer_norm(acc, g_ref[...], beta_ref[...])
    hn = hn * jax.nn.sigmoid(hn)
    o_ref[...] = jnp.dot(hn.astype(_bf16), wpw_ref[...], preferred_element_type=_f32)


def conformer_conv(h, batch, seq, w_dw, b_dw, ln_g, ln_b, w_pw_bf16, tt=512):
    nt = seq // tt
    vec = pl.BlockSpec((1, CONV_CH), lambda b, t: (0, 0))
    return pl.pallas_call(
        _conv_kernel,
        out_shape=jax.ShapeDtypeStruct((batch * seq, CONV_CH), _f32),
        grid=(batch, nt),
        in_specs=[pl.BlockSpec((tt, 2 * CONV_CH), lambda b, t: (b * nt + t, COL_CONV // (2 * CONV_CH))),
                  pl.BlockSpec((CONV_WIDTH, CONV_CH), lambda b, t: (0, 0)),
                  vec, vec, vec,
                  pl.BlockSpec((CONV_CH, CONV_CH), lambda b, t: (0, 0))],
        out_specs=pl.BlockSpec((tt, CONV_CH), lambda b, t: (b * nt + t, 0)),
        scratch_shapes=[pltpu.VMEM((CONV_HALO + tt, CONV_CH), _f32)],
        compiler_params=_cparams(("parallel", "arbitrary")),
        name="conformer_conv",
    )(h, w_dw, b_dw.reshape(1, CONV_CH), ln_g.reshape(1, CONV_CH), ln_b.reshape(1, CONV_CH), w_pw_bf16)


HGRN_LEVELS = (32, 16, 8, 4, 2, 1)
HGRN_TILE = 256


def _hgrn_constants():
    c = HGRN_CHUNK
    t = np.arange(c)[:, None]
    u = np.arange(c)[None, :]
    blocks = [(u <= t), (u > t)]
    masks = []
    for h in HGRN_LEVELS:
        mid = (t // (2 * h)) * (2 * h) + h
        upper = t >= mid
        blocks.append(upper & (u >= mid) & (u <= t))
        blocks.append((~upper) & (u > t) & (u <= mid - 1))
        same = (t // (2 * h)) == (u // (2 * h))
        mid_s = (u // (2 * h)) * (2 * h) + h
        masks.append(same & upper & (u < mid_s))
    masks.append(t == u)
    lmat = np.concatenate(blocks, axis=0).astype(np.float32)
    mask = np.stack(masks).astype(np.float32)
    mask = np.tile(mask, (1, HGRN_HEADS, 1))
    lane_head = np.arange(HGRN_W)[None, :] // HGRN_KDIM
    lane_mask = (lane_head == np.arange(HGRN_HEADS)[:, None]).astype(np.float32)
    bd = (lane_head.T == lane_head).astype(np.float32)
    return lmat, mask, lane_mask, bd


def _split3(x):
    hi = x.astype(_bf16)
    r = x - hi.astype(_f32)
    mid = r.astype(_bf16)
    lo = (r - mid.astype(_f32)).astype(_bf16)
    return hi, mid, lo


def _dot(a, b):
    return jnp.dot(a, b, preferred_element_type=_f32)


def _dot_nt(a, b):
    return lax.dot_general(a, b, (((1,), (1,)), ((), ())), preferred_element_type=_f32)


def _dot_tn(a, b):
    return lax.dot_general(a, b, (((0,), (0,)), ((), ())), preferred_element_type=_f32)


def _exact_dot(l_bf16, x):
    hi, mid, lo = _split3(x)
    return _dot(l_bf16, hi) + _dot(l_bf16, mid) + _dot(l_bf16, lo)


def _hgrn_kernel(q_ref, f_ref, i_ref, g_ref, lb_ref, ng_ref, lmat_ref, mask_ref, lm_ref, bd_ref,
                 o_ref, st_ref):
    c = HGRN_CHUNK
    nl = len(HGRN_LEVELS)

    @pl.when(pl.program_id(1) == 0)
    def _():
        st_ref[...] = jnp.zeros_like(st_ref)

    lb = lb_ref[...]
    bd = bd_ref[...]

    def chunk(ci, carry):
        r0 = pl.multiple_of(ci * c, c)
        rows = pl.ds(r0, c)
        q = q_ref[rows, :]
        f = lb + (1.0 - lb) * jax.nn.sigmoid(f_ref[rows, :])
        logf = jnp.log(f)
        k = 1.0 - f
        v = i_ref[rows, :]
        v_b = v.astype(_bf16)
        ex = jnp.exp(_exact_dot(lmat_ref[...], logf))
        e_cum = ex[0:c]
        e_rest = ex[c:2 * c]

        attn = jnp.zeros((HGRN_HEADS * c, c), _f32)
        for l in range(nl + 1):
            if l < nl:
                ql = q * ex[(2 + 2 * l) * c:(3 + 2 * l) * c]
                kl = k * ex[(3 + 2 * l) * c:(4 + 2 * l) * c]
            else:
                ql, kl = q, k
            qs = jnp.concatenate([ql * lm_ref[h:h + 1, :] for h in range(HGRN_HEADS)], axis=0)
            attn = attn + _dot_nt(qs.astype(_bf16), kl.astype(_bf16)) * mask_ref[l]
        o = jnp.zeros((c, HGRN_W), _f32)
        for h in range(HGRN_HEADS):
            vh = (v * lm_ref[h:h + 1, :]).astype(_bf16)
            o = o + _dot(attn[h * c:(h + 1) * c, :].astype(_bf16), vh)

        st = st_ref[...]
        o = o + _dot_nt((q * e_cum).astype(_bf16), st.astype(_bf16))
        upd = _dot_tn(v_b, (k * e_rest).astype(_bf16)) * bd
        st_ref[...] = st * e_cum[c - 1:c, :] + upd

        osq = o * o
        hi, mid, lo = _split3(osq)
        bd_b = bd.astype(_bf16)
        ms = (_dot(hi, bd_b) + _dot(mid, bd_b) + _dot(lo, bd_b)) * (1.0 / HGRN_VDIM)
        o = o * lax.rsqrt(ms + RMS_EPS)
        o_ref[rows, :] = o * ng_ref[...] * jax.nn.sigmoid(g_ref[rows, :])
        return carry

    lax.fori_loop(0, q_ref.shape[0] // c, chunk, 0)


def hgrn2(h, batch, seq, lb, norm_g):
    lmat, mask, lane_mask, bd = _hgrn_constants()
    tt = HGRN_TILE
    nt = seq // tt

    def col(cb):
        return pl.BlockSpec((tt, HGRN_W), lambda b, t: (b * nt + t, cb))

    def const(shape):
        nd = len(shape)
        return pl.BlockSpec(shape, lambda b, t: (0,) * nd)

    return pl.pallas_call(
        _hgrn_kernel,
        out_shape=jax.ShapeDtypeStruct((batch * seq, HGRN_W), _f32),
        grid=(batch, nt),
        in_specs=[col(COL_QH // HGRN_W), col(COL_FH // HGRN_W), col(COL_IH // HGRN_W), col(COL_GH // HGRN_W),
                  const((1, HGRN_W)), const((1, HGRN_W)),
                  const(lmat.shape), const(mask.shape), const(lane_mask.shape), const(bd.shape)],
        out_specs=pl.BlockSpec((tt, HGRN_W), lambda b, t: (b * nt + t, 0)),
        scratch_shapes=[pltpu.VMEM((HGRN_W, HGRN_W), _f32)],
        compiler_params=_cparams(("parallel", "arbitrary")),
        name="hgrn2",
    )(h, h, h, h, lb.reshape(1, HGRN_W), norm_g.reshape(1, HGRN_W),
      jnp.asarray(lmat, _bf16), jnp.asarray(mask), jnp.asarray(lane_mask), jnp.asarray(bd))


def _rel_bucket_np(dist):
    n = np.maximum(dist, 0)
    nf = np.maximum(n, 1).astype(np.float32)
    large = REL_MAX_EXACT + (np.log(nf / np.float32(REL_MAX_EXACT)) / np.float32(math.log(REL_MAX_DIST / REL_MAX_EXACT))
                             * np.float32(REL_BUCKETS - REL_MAX_EXACT)).astype(np.int32)
    return np.where(n < REL_MAX_EXACT, n, np.minimum(large, REL_BUCKETS - 1)).astype(np.int32)


FAR_DIST = 113
assert int(_rel_bucket_np(np.arange(FAR_DIST, 4 * FAR_DIST)).min()) == REL_BUCKETS - 1


LOG2E = 1.4426950408889634
M_FLOOR = -5e29


def _bias_rel(tab, dist):
    dist = np.asarray(dist)
    vals = jnp.take(tab, jnp.asarray(_rel_bucket_np(dist).reshape(-1)), axis=1)
    vals = (vals - tab[:, REL_BUCKETS - 1:]) * LOG2E
    vals = vals.reshape((tab.shape[0],) + dist.shape)
    return jnp.where(jnp.asarray(dist >= 0)[None], vals, NEG)


def _toeplitz(g, rows, width):
    nh, length = g.shape
    flat = jnp.tile(g, (1, rows))[:, :rows * (length - 1)]
    return flat.reshape(nh, rows, length - 1)[:, :, :width]


def _near_bias(tab):
    k = np.arange(2 * (PREV + TQ))
    g = _bias_rel(tab, np.where(k < PREV + TQ, PREV - k, 4 * FAR_DIST))
    return _toeplitz(g, TQ, PREV + TQ)


FAR_TILE = 512
FAR_UNROLL = 4


def _softmax_update(m, acc, s, rhs_b):
    m_new = jnp.maximum(m, jnp.max(s, axis=-1, keepdims=True))
    alpha = jnp.exp2(m - m_new)
    p = jnp.exp2(s - m_new)
    return m_new, alpha * acc + _dot(p.astype(_bf16), rhs_b)


def _block_onehot_neg(seq, block):
    k = lax.broadcasted_iota(jnp.int32, (seq, 128), 0)
    lane = lax.broadcasted_iota(jnp.int32, (seq, 128), 1)
    return jnp.where(lane == k // block, NEG, 0.0).astype(_bf16)


MOBA_UNIT = PREV


def _moba_kernel(q_ref, k_ref, v_ref, et_ref, bn_ref, o_ref, kmean_ref):
    c = pl.program_id(2)
    seq = k_ref.shape[0]
    nb = seq // MOBA_BLOCK
    tq = TQ

    @pl.when(c == 0)
    def _():
        kmean_ref[...] = jnp.sum(k_ref[...].astype(_f32).reshape(nb, MOBA_BLOCK, 128), axis=1) * (1.0 / MOBA_BLOCK)

    s0 = c * tq
    own = s0 // MOBA_BLOCK
    far_end = s0 - PREV
    n_far = (jnp.maximum(far_end, 0) + FAR_TILE - 1) // FAR_TILE

    lane = lax.broadcasted_iota(jnp.int32, (1, 128), 1)
    jcol = lax.broadcasted_iota(jnp.int32, (nb, tq), 0)
    rep = (lax.broadcasted_iota(jnp.int32, (nb, 128), 1) // (MOBA_BLOCK // MOBA_UNIT)
           == lax.broadcasted_iota(jnp.int32, (nb, 128), 0)).astype(_bf16)

    q_all = q_ref[...] * (HEAD_DIM ** -0.5 * LOG2E)
    kmean_b = kmean_ref[...].astype(_bf16)
    q_far, q_near = [], []
    for hh in range(2):
        qh = jnp.where(lane // HEAD_DIM == hh, q_all, 0.0).astype(_bf16)
        gate = _dot_nt(kmean_b, qh)
        work = jnp.where(jcol < own, gate, -BIG)
        sel = jnp.zeros((nb, tq), _f32)
        for _ in range(MOBA_TOPK):
            mx = jnp.max(work, axis=0, keepdims=True)
            idx = jnp.min(jnp.where(work == mx, jcol, nb), axis=0, keepdims=True)
            pick = jcol == idx
            sel = jnp.where(pick & (mx > -0.5 * BIG), 1.0, sel)
            work = jnp.where(pick, -jnp.inf, work)
        sel = jnp.where(jcol == own, 1.0, sel)
        notsel = _dot_tn((1.0 - sel).astype(_bf16), rep)
        notsel_far = jnp.where(lane < far_end // MOBA_UNIT, notsel, 1.0)
        q_far.append(jnp.concatenate([qh, notsel_far.astype(_bf16)], axis=1))
        q_near.append(jnp.concatenate([qh, notsel.astype(_bf16)], axis=1))
    qf = jnp.concatenate(q_far, axis=0)
    qn = jnp.concatenate(q_near, axis=0)
    ones = jnp.ones((FAR_TILE, 128), _bf16)

    def far_body(it, carry):
        for u in range(FAR_UNROLL):
            rows = pl.ds(pl.multiple_of((it * FAR_UNROLL + u) * FAR_TILE, FAR_TILE), FAR_TILE)
            s = _dot_nt(qf, jnp.concatenate([k_ref[rows, :], et_ref[rows, :]], axis=1))
            carry = _softmax_update(*carry, s, jnp.concatenate([v_ref[rows, :], ones], axis=1))
        return carry

    init = (jnp.full((2 * tq, 1), M_FLOOR, _f32), jnp.zeros((2 * tq, 256), _f32))
    m, acc = lax.fori_loop(0, (n_far + FAR_UNROLL - 1) // FAR_UNROLL, far_body, init)

    prev = pl.ds(pl.multiple_of(jnp.maximum(far_end, 0), PREV), PREV)
    diag = pl.ds(pl.multiple_of(s0, tq), tq)
    k_n = jnp.concatenate([jnp.concatenate([k_ref[prev, :], et_ref[prev, :]], axis=1),
                           jnp.concatenate([k_ref[diag, :], et_ref[diag, :]], axis=1)], axis=0)
    v_n = jnp.concatenate([jnp.concatenate([v_ref[prev, :], v_ref[diag, :]], axis=0), ones[:PREV + tq]], axis=1)
    col = lax.broadcasted_iota(jnp.int32, (1, 1, PREV + tq), 2)
    nbias = jnp.where((col < PREV) & (c == 0), NEG, bn_ref[...])
    s = _dot_nt(qn, k_n) + nbias.reshape(2 * tq, PREV + tq)
    _, acc = _softmax_update(m, acc, s, v_n)
    out = acc[:, :128] / acc[:, 128:]
    o_ref[...] = jnp.where(lane // HEAD_DIM == 0, out[:tq], out[tq:])


def moba_attention(h, hb, batch, seq, bias_near):
    nq = seq // TQ
    assert seq // MOBA_UNIT <= 128 and MOBA_BLOCK % TQ == 0 and FAR_TILE >= PREV + TQ
    et = _block_onehot_neg(seq, MOBA_UNIT)

    def kv(col):
        return pl.BlockSpec((seq, 128), lambda b, hp, c: (b, col // 128 + hp))

    return pl.pallas_call(
        _moba_kernel,
        out_shape=jax.ShapeDtypeStruct((batch * seq, MOBA_W), _f32),
        grid=(batch, MOBA_HEADS // 2, nq),
        in_specs=[pl.BlockSpec((TQ, 128), lambda b, hp, c: (b * nq + c, COL_QM // 128 + hp)),
                  kv(COL_KM), kv(COL_VM),
                  pl.BlockSpec((seq, 128), lambda b, hp, c: (0, 0)),
                  pl.BlockSpec((2, TQ, PREV + TQ), lambda b, hp, c: (hp, 0, 0))],
        out_specs=pl.BlockSpec((TQ, 128), lambda b, hp, c: (b * nq + c, hp)),
        scratch_shapes=[pltpu.VMEM((seq // MOBA_BLOCK, 128), _f32)],
        compiler_params=_cparams(("parallel", "parallel", "arbitrary")),
        name="moba_attention",
    )(h, hb, hb, et, bias_near)


def _compress_kernel(a_ref, pet_ref, peb_ref, wt_ref, wb_ref, b1_ref, w2_ref, o_ref):
    a = a_ref[...]
    nrow = a.shape[0]
    top = _dot((a + pet_ref[...]).astype(_bf16), wt_ref[...])
    bot = _dot((a + peb_ref[...]).astype(_bf16), wb_ref[...])
    pre = top + pltpu.roll(bot, nrow - 1, 0) + b1_ref[...]
    hid = jax.nn.gelu(pre)
    o_ref[...] = _dot(hid.astype(_bf16), w2_ref[...])


def nsa_compress(h, batch, seq, pe, w1, b1, w2):
    nr = seq // NSA_CMP_STRIDE
    half = NSA_CMP_LEN // 2
    a = h[:, COL_KVC:COL_KVC + 128].reshape(batch, nr, NSA_CMP_STRIDE * 128)
    pe_cat = jnp.concatenate([pe[0], pe[1]], axis=-1)
    pe_top = pe_cat[:half].reshape(1, half * 128)
    pe_bot = pe_cat[half:].reshape(1, half * 128)
    hid = NSA_CMP_HIDDEN
    w_all = jnp.zeros((NSA_CMP_LEN, 128, 2 * hid), _f32)
    w_all = w_all.at[:, :HEAD_DIM, :hid].set(w1[0].reshape(NSA_CMP_LEN, HEAD_DIM, hid))
    w_all = w_all.at[:, HEAD_DIM:, hid:].set(w1[1].reshape(NSA_CMP_LEN, HEAD_DIM, hid))
    w_top = w_all[:half].reshape(half * 128, 2 * hid).astype(_bf16)
    w_bot = w_all[half:].reshape(half * 128, 2 * hid).astype(_bf16)
    b1cat = jnp.concatenate([b1[0], b1[1]]).reshape(1, 2 * hid)
    w2bd = jnp.zeros((2 * hid, 128), _f32)
    w2bd = w2bd.at[:hid, :HEAD_DIM].set(w2[0]).at[hid:, HEAD_DIM:].set(w2[1]).astype(_bf16)

    def const(shape):
        return pl.BlockSpec(shape, lambda b: (0,) * len(shape))

    return pl.pallas_call(
        _compress_kernel,
        out_shape=jax.ShapeDtypeStruct((batch, nr, 128), _f32),
        grid=(batch,),
        in_specs=[pl.BlockSpec((None, nr, half * 128), lambda b: (b, 0, 0)),
                  const(pe_top.shape), const(pe_bot.shape), const(w_top.shape), const(w_bot.shape),
                  const(b1cat.shape), const(w2bd.shape)],
        out_specs=pl.BlockSpec((None, nr, 128), lambda b: (b, 0, 0)),
        compiler_params=_cparams(("parallel",)),
        name="nsa_compress",
    )(a, pe_top, pe_bot, w_top, w_bot, b1cat, w2bd)


CMP_FIRST = PREV // NSA_CMP_STRIDE
CMP_NEAR = -(-((TQ + PREV - NSA_CMP_LEN) // NSA_CMP_STRIDE + 1) // 8) * 8
NSA_WKEYS = NSA_WINDOW + TQ
NSA_WTILES = NSA_WKEYS // PREV


def _nsa_tables(tab, seq):
    nc = seq // NSA_CMP_STRIDE
    nsel = seq // NSA_SEL_BLOCK
    n_cmp = (seq - NSA_CMP_LEN) // NSA_CMP_STRIDE + 1
    i = np.arange(TQ)[:, None]
    m = np.arange(CMP_NEAR)[None, :]
    d_c = i + (PREV - NSA_CMP_LEN + 1) - NSA_CMP_STRIDE * m
    cmp_bias = _bias_rel(tab, np.maximum(d_c, 0))
    k = np.arange(2 * NSA_WKEYS)
    g = _bias_rel(tab, np.where((k >= 1) & (k <= NSA_WINDOW), NSA_WINDOW - k, -1))
    win_bias = _toeplitz(g, TQ, NSA_WKEYS)
    n = np.arange(nc)[:, None]
    js = np.arange(nsel)[None, :]
    cs, ce = n * NSA_CMP_STRIDE, n * NSA_CMP_STRIDE + NSA_CMP_LEN
    ss = js * NSA_SEL_BLOCK
    ov = np.clip(np.minimum(ce, ss + NSA_SEL_BLOCK) - np.maximum(cs, ss), 0, None).astype(np.float32) / NSA_CMP_LEN
    ov = np.where(n < n_cmp, ov, 0.0)
    return cmp_bias, win_bias, _near_bias(tab), jnp.asarray(ov.T, _bf16)


def _exact_dot_r(x, r_bf16):
    hi, mid, lo = _split3(x)
    return _dot(hi, r_bf16) + _dot(mid, r_bf16) + _dot(lo, r_bf16)


def _nsa_kernel(q_ref, gn_ref, kvc_ref, kvs_ref, kvw_ref, et_ref, cb_ref, wb_ref, nb_ref, ovt_ref, o_ref):
    c = pl.program_id(1)
    tq = TQ
    nh = NSA_HEADS
    nc = kvc_ref.shape[0]
    nsel = ovt_ref.shape[0]
    s0 = c * tq
    far_end = s0 - PREV
    lane = lax.broadcasted_iota(jnp.int32, (1, 128), 1)
    low = lane < HEAD_DIM

    q_all = q_ref[...] * (HEAD_DIM ** -0.5 * LOG2E)
    qh = []
    for h in range(nh):
        blk = q_all[:, 128 * (h // 2):128 * (h // 2) + 128]
        if h % 2 == 1:
            blk = pltpu.roll(blk, HEAD_DIM, 1)
        qh.append(jnp.where(low, blk, 0.0))
    q_b = jnp.concatenate(qh, axis=0).astype(_bf16)

    kvc_b = kvc_ref[...].astype(_bf16)
    s_c = _dot_nt(q_b, kvc_b).reshape(nh, tq, nc)
    n_idx = lax.broadcasted_iota(jnp.int32, (tq, nc), 1)
    t_idx = lax.broadcasted_iota(jnp.int32, (tq, nc), 0) + s0
    valid_c = (n_idx * NSA_CMP_STRIDE + (NSA_CMP_LEN - 1)) <= t_idx
    m_sel = lax.broadcasted_iota(jnp.int32, (CMP_NEAR, nc), 0)
    n_sel = lax.broadcasted_iota(jnp.int32, (CMP_NEAR, nc), 1)
    place = (n_sel == (s0 // NSA_CMP_STRIDE) - CMP_FIRST + m_sel).astype(_bf16)
    psum = jnp.zeros((tq, nc), _f32)
    o_c = []
    for h in range(nh):
        s = s_c[h] + _exact_dot_r(cb_ref[h], place)
        s = jnp.where(valid_c, s, NEG)
        mx = jnp.max(s, axis=1, keepdims=True)
        e = jnp.where(valid_c, jnp.exp2(s - mx), 0.0)
        p = e / jnp.maximum(jnp.sum(e, axis=1, keepdims=True), 1e-30)
        psum = psum + p
        o_c.append(_dot(p.astype(_bf16), kvc_b))

    imp_t = jnp.zeros((nsel, tq), _f32)
    for part in _split3(psum):
        imp_t = imp_t + _dot_nt(ovt_ref[...], part)
    jblk = lax.broadcasted_iota(jnp.int32, (nsel, tq), 0)
    blk_t = (lax.broadcasted_iota(jnp.int32, (nsel, tq), 1) + s0) // NSA_SEL_BLOCK
    forced = (jblk == 0) | (jblk == blk_t) | (jblk == blk_t - 1)
    work = jnp.where(forced, BIG, jnp.where(jblk <= blk_t, imp_t, -BIG))
    sel = jnp.zeros((nsel, tq), _f32)
    for _ in range(min(NSA_TOPN, nsel)):
        mx = jnp.max(work, axis=0, keepdims=True)
        idx = jnp.min(jnp.where(work == mx, jblk, nsel), axis=0, keepdims=True)
        pick = jblk == idx
        sel = jnp.where(pick, 1.0, sel)
        work = jnp.where(pick, -jnp.inf, work)

    eye = (lax.broadcasted_iota(jnp.int32, (nsel, 128), 0)
           == lax.broadcasted_iota(jnp.int32, (nsel, 128), 1)).astype(_bf16)
    notsel = _dot_tn((1.0 - sel).astype(_bf16), eye)
    notsel_far = jnp.where(lane < far_end // NSA_SEL_BLOCK, notsel, 1.0)
    qf = jnp.concatenate([q_b, jnp.concatenate([notsel_far.astype(_bf16)] * nh, axis=0)], axis=1)
    qn = jnp.concatenate([q_b, jnp.concatenate([notsel.astype(_bf16)] * nh, axis=0)], axis=1)

    def ones_v(kv):
        return jnp.where(low, 1.0, kv).astype(_bf16)

    def far_body(it, carry):
        for u in range(FAR_UNROLL):
            rows = pl.ds(pl.multiple_of((it * FAR_UNROLL + u) * FAR_TILE, FAR_TILE), FAR_TILE)
            kv = kvs_ref[rows, :]
            s = _dot_nt(qf, jnp.concatenate([kv, et_ref[rows, :]], axis=1))
            carry = _softmax_update(*carry, s, ones_v(kv))
        return carry

    n_far = (jnp.maximum(far_end, 0) + FAR_TILE - 1) // FAR_TILE
    init = (jnp.full((nh * tq, 1), M_FLOOR, _f32), jnp.zeros((nh * tq, 128), _f32))
    m, acc = lax.fori_loop(0, (n_far + FAR_UNROLL - 1) // FAR_UNROLL, far_body, init)

    prev = pl.ds(pl.multiple_of(jnp.maximum(far_end, 0), PREV), PREV)
    diag = pl.ds(pl.multiple_of(s0, tq), tq)
    kv = jnp.concatenate([kvs_ref[prev, :], kvs_ref[diag, :]], axis=0)
    et = jnp.concatenate([et_ref[prev, :], et_ref[diag, :]], axis=0)
    col = lax.broadcasted_iota(jnp.int32, (1, 1, PREV + tq), 2)
    nbias = jnp.where((col < PREV) & (c == 0), NEG, nb_ref[...])
    s = _dot_nt(qn, jnp.concatenate([kv, et], axis=1)) + nbias.reshape(nh * tq, PREV + tq)
    _, acc = _softmax_update(m, acc, s, ones_v(kv))
    o_s = (acc / acc[:, 0:1]).reshape(nh, tq, 128)

    tiles = []
    for r in range(NSA_WTILES):
        start = pl.multiple_of(jnp.maximum(s0 - NSA_WINDOW + r * PREV, 0), PREV)
        tiles.append(kvw_ref[pl.ds(start, PREV), :])
    kvw_b = jnp.concatenate(tiles, axis=0)
    wk = NSA_WKEYS
    kpos_w = lax.broadcasted_iota(jnp.int32, (1, 1, wk), 2) + (s0 - NSA_WINDOW)
    s = _dot_nt(q_b, kvw_b).reshape(nh, tq, wk) + jnp.where(kpos_w >= 0, wb_ref[...], NEG)
    e = jnp.exp2(s - jnp.max(s, axis=2, keepdims=True))
    acc_w = _dot(e.reshape(nh * tq, wk).astype(_bf16), ones_v(kvw_b))
    o_w = (acc_w / acc_w[:, 0:1]).reshape(nh, tq, 128)

    g = jax.nn.sigmoid(gn_ref[...])
    outs = []
    for h in range(nh):
        outs.append(g[:, 3 * h:3 * h + 1] * o_c[h] + g[:, 3 * h + 1:3 * h + 2] * o_s[h]
                    + g[:, 3 * h + 2:3 * h + 3] * o_w[h])
    for hp in range(nh // 2):
        o_ref[:, 128 * hp:128 * hp + 128] = jnp.where(low, pltpu.roll(outs[2 * hp], HEAD_DIM, 1), outs[2 * hp + 1])


def nsa_attention(h, hb, kvc, batch, seq, tables):
    nq = seq // TQ
    nc = seq // NSA_CMP_STRIDE
    cmp_bias, win_bias, near_bias, ov_t = tables
    assert seq // NSA_SEL_BLOCK <= 128
    et = _block_onehot_neg(seq, NSA_SEL_BLOCK)

    def const(shape):
        return pl.BlockSpec(shape, lambda b, c: (0,) * len(shape))

    return pl.pallas_call(
        _nsa_kernel,
        out_shape=jax.ShapeDtypeStruct((batch * seq, NSA_W), _f32),
        grid=(batch, nq),
        in_specs=[pl.BlockSpec((TQ, NSA_W), lambda b, c: (b * nq + c, COL_QN // NSA_W)),
                  pl.BlockSpec((TQ, 128), lambda b, c: (b * nq + c, COL_GN // 128)),
                  pl.BlockSpec((None, nc, 128), lambda b, c: (b, 0, 0)),
                  pl.BlockSpec((seq, 128), lambda b, c: (b, COL_KVS // 128)),
                  pl.BlockSpec((seq, 128), lambda b, c: (b, COL_KVW // 128)),
                  const(et.shape),
                  const(cmp_bias.shape), const(win_bias.shape), const(near_bias.shape), const(ov_t.shape)],
        out_specs=pl.BlockSpec((TQ, NSA_W), lambda b, c: (b * nq + c, 0)),
        compiler_params=_cparams(("parallel", "arbitrary")),
        name="nsa_attention",
    )(h, h, kvc, hb, hb, et, cmp_bias, win_bias, near_bias, ov_t)


def kernel(x, w_in, nsa_cmp_pe, nsa_cmp_w1, nsa_cmp_b1, nsa_cmp_w2, conv_dw_w, conv_dw_b, conv_ln_g,
           conv_ln_b, conv_pw_w, hgrn_lb_logits, hgrn_norm_g, w_out, ln1_g, ln1_b, w_ff1, w_ff2,
           ln2_g, ln2_b, rel_bias):
    batch, seq, d = x.shape
    lb_sm = jax.nn.softmax(hgrn_lb_logits.astype(_f32), axis=0)
    lbs = jnp.cumsum(lb_sm, axis=0) - lb_sm[0]
    x2 = x.reshape(batch * seq, d)
    nsa_tables = _nsa_tables(rel_bias[:NSA_HEADS], seq)
    moba_near = _near_bias(rel_bias[NSA_HEADS:])
    for l in range(DEPTH):
        h, hb = in_proj(x2, _regroup_w_in(w_in[l]))
        o_conv = conformer_conv(h, batch, seq, conv_dw_w[l], conv_dw_b[l], conv_ln_g[l], conv_ln_b[l],
                                conv_pw_w[l].astype(_bf16))
        o_hgrn = hgrn2(h, batch, seq, lbs[l], hgrn_norm_g[l])
        kvc = nsa_compress(h, batch, seq, nsa_cmp_pe[l], nsa_cmp_w1[l], nsa_cmp_b1[l], nsa_cmp_w2[l])
        o_nsa = nsa_attention(h, hb, kvc, batch, seq, nsa_tables)
        o_moba = moba_attention(h, hb, batch, seq, moba_near)
        x1 = out_proj_ln(o_nsa, o_moba, o_conv, o_hgrn, x2, w_out[l].astype(_bf16), ln1_g[l], ln1_b[l])
        x2 = mlp_ln(x1, w_ff1[l].astype(_bf16), w_ff2[l].astype(_bf16), ln2_g[l], ln2_b[l])
    return x2.reshape(batch, seq, d)
```

```python
import math

import jax
import jax.numpy as jnp
import numpy as np
from jax import lax
from jax.experimental import pallas as pl
from jax.experimental.pallas import tpu as pltpu

D_MODEL = 1024
DEPTH = 2
HEAD_DIM = 64
NSA_HEADS = 4
NSA_W = NSA_HEADS * HEAD_DIM
NSA_CMP_LEN = 32
NSA_CMP_STRIDE = 16
NSA_CMP_HIDDEN = 256
NSA_SEL_BLOCK = 64
NSA_TOPN = 16
NSA_WINDOW = 512
MOBA_HEADS = 4
MOBA_W = MOBA_HEADS * HEAD_DIM
MOBA_BLOCK = 256
MOBA_TOPK = 3
CONV_CH = 256
CONV_WIDTH = 31
HGRN_HEADS = 4
HGRN_KDIM = 64
HGRN_VDIM = 64
HGRN_W = HGRN_HEADS * HGRN_VDIM
HGRN_CHUNK = 64
MIX_WIDTH = NSA_W + MOBA_W + CONV_CH + HGRN_W
IN_WIDTHS = (NSA_W, HEAD_DIM, HEAD_DIM, HEAD_DIM, HEAD_DIM, HEAD_DIM, HEAD_DIM, 3 * NSA_HEADS,
             MOBA_W, MOBA_W, MOBA_W, 2 * CONV_CH, HGRN_W, HGRN_W, HGRN_W, HGRN_W)
D_FF = 4 * D_MODEL
REL_BUCKETS = 32
REL_MAX_EXACT = 16
REL_MAX_DIST = 128
TQ = 256
PREV = 128
LN_EPS = 1e-5
RMS_EPS = 1e-6
BIG = 1e9
DEEPNORM_ALPHA = (2 * DEPTH) ** 0.25

H_COLS = 3072
COL_QN = 0
COL_KVC = 256
COL_KVS = 384
COL_KVW = 512
COL_GN = 640
COL_QM = 768
COL_KM = 1024
COL_VM = 1280
COL_CONV = 1536
COL_QH = 2048
COL_FH = 2304
COL_IH = 2560
COL_GH = 2816

VMEM_LIMIT = 48 * 1024 * 1024
NEG = -1e30

_f32 = jnp.float32
_bf16 = jnp.bfloat16


def _cparams(sem):
    return pltpu.CompilerParams(dimension_semantics=sem, vmem_limit_bytes=VMEM_LIMIT)


def _layer_norm(z, g, b):
    mu = jnp.mean(z, axis=-1, keepdims=True)
    zc = z - mu
    var = jnp.mean(zc * zc, axis=-1, keepdims=True)
    return zc * lax.rsqrt(var + LN_EPS) * g + b


def _in_proj_kernel(x_ref, w_ref, o_ref, ob_ref):
    acc = jnp.dot(x_ref[...].astype(_bf16), w_ref[...], preferred_element_type=_f32)
    o_ref[...] = acc
    ob_ref[...] = acc.astype(_bf16)


def in_proj(x2, w_bf16, tm=512):
    n, d = x2.shape
    cols = w_bf16.shape[1]
    out = pl.BlockSpec((tm, cols), lambda i: (i, 0))
    return pl.pallas_call(
        _in_proj_kernel,
        out_shape=(jax.ShapeDtypeStruct((n, cols), _f32), jax.ShapeDtypeStruct((n, cols), _bf16)),
        grid=(n // tm,),
        in_specs=[pl.BlockSpec((tm, d), lambda i: (i, 0)),
                  pl.BlockSpec((d, cols), lambda i: (0, 0))],
        out_specs=(out, out),
        compiler_params=_cparams(("parallel",)),
        name="in_proj",
    )(x2, w_bf16)


def _regroup_w_in(w_in_l):
    gate_end = COL_GN + 3 * NSA_HEADS
    src = lax.broadcasted_iota(jnp.int32, (gate_end + H_COLS - COL_QM, H_COLS), 0)
    dst = lax.broadcasted_iota(jnp.int32, (gate_end + H_COLS - COL_QM, H_COLS), 1)
    place = (dst == jnp.where(src < gate_end, src, src + (COL_QM - gate_end))).astype(_bf16)
    assert place.shape[0] == w_in_l.shape[1] == sum(IN_WIDTHS)
    return jnp.dot(w_in_l.astype(_bf16), place, preferred_element_type=_f32).astype(_bf16)


def _out_proj_kernel(a_ref, b_ref, c_ref, d_ref, x_ref, w_ref, g_ref, beta_ref, o_ref):
    mixed = jnp.dot(a_ref[...].astype(_bf16), w_ref[0:256, :], preferred_element_type=_f32)
    mixed += jnp.dot(b_ref[...].astype(_bf16), w_ref[256:512, :], preferred_element_type=_f32)
    mixed += jnp.dot(c_ref[...].astype(_bf16), w_ref[512:768, :], preferred_element_type=_f32)
    mixed += jnp.dot(d_ref[...].astype(_bf16), w_ref[768:1024, :], preferred_element_type=_f32)
    z = DEEPNORM_ALPHA * x_ref[...] + mixed
    o_ref[...] = _layer_norm(z, g_ref[...], beta_ref[...])


def out_proj_ln(o_nsa, o_moba, o_conv, o_hgrn, x2, w_out_bf16, g, b, tm=512):
    n, d = x2.shape
    part = pl.BlockSpec((tm, 256), lambda i: (i, 0))
    vec = pl.BlockSpec((1, d), lambda i: (0, 0))
    return pl.pallas_call(
        _out_proj_kernel,
        out_shape=jax.ShapeDtypeStruct((n, d), _f32),
        grid=(n // tm,),
        in_specs=[part, part, part, part,
                  pl.BlockSpec((tm, d), lambda i: (i, 0)),
                  pl.BlockSpec((MIX_WIDTH, d), lambda i: (0, 0)),
                  vec, vec],
        out_specs=pl.BlockSpec((tm, d), lambda i: (i, 0)),
        compiler_params=_cparams(("parallel",)),
        name="out_proj_ln",
    )(o_nsa, o_moba, o_conv, o_hgrn, x2, w_out_bf16, g.reshape(1, d), b.reshape(1, d))


def _mlp_kernel(x_ref, w1_ref, w2_ref, g_ref, beta_ref, o_ref, acc_ref):
    j = pl.program_id(1)

    @pl.when(j == 0)
    def _():
        acc_ref[...] = jnp.zeros_like(acc_ref)

    hid = jnp.dot(x_ref[...].astype(_bf16), w1_ref[...], preferred_element_type=_f32)
    hid = jnp.square(jnp.maximum(hid, 0.0))
    acc_ref[...] += jnp.dot(hid.astype(_bf16), w2_ref[...], preferred_element_type=_f32)

    @pl.when(j == pl.num_programs(1) - 1)
    def _():
        z = DEEPNORM_ALPHA * x_ref[...] + acc_ref[...]
        o_ref[...] = _layer_norm(z, g_ref[...], beta_ref[...])


def mlp_ln(x1, w1_bf16, w2_bf16, g, b, tm=1024, tf=1024):
    n, d = x1.shape
    dff = w1_bf16.shape[1]
    vec = pl.BlockSpec((1, d), lambda i, j: (0, 0))
    return pl.pallas_call(
        _mlp_kernel,
        out_shape=jax.ShapeDtypeStruct((n, d), _f32),
        grid=(n // tm, dff // tf),
        in_specs=[pl.BlockSpec((tm, d), lambda i, j: (i, 0)),
                  pl.BlockSpec((d, tf), lambda i, j: (0, j)),
                  pl.BlockSpec((tf, d), lambda i, j: (j, 0)),
                  vec, vec],
        out_specs=pl.BlockSpec((tm, d), lambda i, j: (i, 0)),
        scratch_shapes=[pltpu.VMEM((tm, d), _f32)],
        compiler_params=_cparams(("parallel", "arbitrary")),
        name="mlp_ln",
    )(x1, w1_bf16, w2_bf16, g.reshape(1, d), b.reshape(1, d))


CONV_HALO = 32


def _conv_kernel(u_ref, wdw_ref, bdw_ref, g_ref, beta_ref, wpw_ref, o_ref, buf_ref):
    t = pl.program_id(1)
    tt = u_ref.shape[0]

    @pl.when(t == 0)
    def _():
        buf_ref[0:CONV_HALO, :] = jnp.zeros((CONV_HALO, CONV_CH), _f32)

    @pl.when(t > 0)
    def _():
        buf_ref[0:CONV_HALO, :] = buf_ref[tt:tt + CONV_HALO, :]

    u = u_ref[...]
    a = u[:, :CONV_CH]
    gl = u[:, CONV_CH:]
    buf_ref[CONV_HALO:CONV_HALO + tt, :] = a * jax.nn.sigmoid(gl)

    acc = jnp.zeros((tt, CONV_CH), _f32) + bdw_ref[...]
    for w in range(CONV_WIDTH):
        off = CONV_HALO - (CONV_WIDTH - 1) + w
        acc = acc + buf_ref[off:off + tt, :] * wdw_ref[w:w + 1, :]
    hn = _layer_norm(acc, g_ref[...], beta_ref[...])
    hn = hn * jax.nn.sigmoid(hn)
    o_ref[...] = jnp.dot(hn.astype(_bf16), wpw_ref[...], preferred_element_type=_f32)


def conformer_conv(h, batch, seq, w_dw, b_dw, ln_g, ln_b, w_pw_bf16, tt=512):
    nt = seq // tt
    vec = pl.BlockSpec((1, CONV_CH), lambda b, t: (0, 0))
    return pl.pallas_call(
        _conv_kernel,
        out_shape=jax.ShapeDtypeStruct((batch * seq, CONV_CH), _f32),
        grid=(batch, nt),
        in_specs=[pl.BlockSpec((tt, 2 * CONV_CH), lambda b, t: (b * nt + t, COL_CONV // (2 * CONV_CH))),
                  pl.BlockSpec((CONV_WIDTH, CONV_CH), lambda b, t: (0, 0)),
                  vec, vec, vec,
                  pl.BlockSpec((CONV_CH, CONV_CH), lambda b, t: (0, 0))],
        out_specs=pl.BlockSpec((tt, CONV_CH), lambda b, t: (b * nt + t, 0)),
        scratch_shapes=[pltpu.VMEM((CONV_HALO + tt, CONV_CH), _f32)],
        compiler_params=_cparams(("parallel", "arbitrary")),
        name="conformer_conv",
    )(h, w_dw, b_dw.reshape(1, CONV_CH), ln_g.reshape(1, CONV_CH), ln_b.reshape(1, CONV_CH), w_pw_bf16)


HGRN_LEVELS = (32, 16, 8, 4, 2, 1)
HGRN_TILE = 256


def _hgrn_constants():
    c = HGRN_CHUNK
    t = np.arange(c)[:, None]
    u = np.arange(c)[None, :]
    blocks = [(u <= t), (u > t)]
    masks = []
    for h in HGRN_LEVELS:
        mid = (t // (2 * h)) * (2 * h) + h
        upper = t >= mid
        blocks.append((upper & (u >= mid) & (u <= t)) | ((~upper) & (u > t) & (u <= mid - 1)))
        same = (t // (2 * h)) == (u // (2 * h))
        mid_s = (u // (2 * h)) * (2 * h) + h
        masks.append(same & upper & (u < mid_s))
    masks.append(t == u)
    lmat = np.concatenate(blocks, axis=0).astype(np.float32)
    mask = np.stack(masks).astype(np.float32)
    mask = np.tile(mask, (1, HGRN_HEADS, 1))
    lane_head = np.arange(HGRN_W)[None, :] // HGRN_KDIM
    lane_mask = (lane_head == np.arange(HGRN_HEADS)[:, None]).astype(np.float32)
    bd = (lane_head.T == lane_head).astype(np.float32)
    return lmat, mask, lane_mask, bd


def _split3(x):
    hi = x.astype(_bf16)
    r = x - hi.astype(_f32)
    mid = r.astype(_bf16)
    lo = (r - mid.astype(_f32)).astype(_bf16)
    return hi, mid, lo


def _dot(a, b):
    return jnp.dot(a, b, preferred_element_type=_f32)


def _dot_nt(a, b):
    return lax.dot_general(a, b, (((1,), (1,)), ((), ())), preferred_element_type=_f32)


def _dot_tn(a, b):
    return lax.dot_general(a, b, (((0,), (0,)), ((), ())), preferred_element_type=_f32)


def _exact_dot(l_bf16, x):
    hi = x.astype(_bf16)
    lo = (x - hi.astype(_f32)).astype(_bf16)
    return _dot(l_bf16, hi) + _dot(l_bf16, lo)


def _hgrn_kernel(q_ref, f_ref, i_ref, g_ref, lb_ref, ng_ref, lmat_ref, mask_ref, lm_ref, bd_ref,
                 o_ref, st_ref):
    c = HGRN_CHUNK
    nl = len(HGRN_LEVELS)

    @pl.when(pl.program_id(1) == 0)
    def _():
        st_ref[...] = jnp.zeros_like(st_ref)

    lb = lb_ref[...]
    bd = bd_ref[...]

    def chunk(ci):
        rows = pl.ds(ci * c, c)
        q = q_ref[rows, :]
        f = lb + (1.0 - lb) * jax.nn.sigmoid(f_ref[rows, :])
        logf = jnp.log(f)
        k = 1.0 - f
        v = i_ref[rows, :]
        v_b = v.astype(_bf16)
        ex = jnp.exp(_exact_dot(lmat_ref[...], logf))
        e_cum = ex[0:c]
        e_rest = ex[c:2 * c]

        attn = jnp.zeros((HGRN_HEADS * c, c), _f32)
        for l in range(nl + 1):
            if l < nl:
                el = ex[(2 + l) * c:(3 + l) * c]
                ql, kl = q * el, k * el
            else:
                ql, kl = q, k
            qs = jnp.concatenate([ql * lm_ref[h:h + 1, :] for h in range(HGRN_HEADS)], axis=0)
            attn = attn + _dot_nt(qs.astype(_bf16), kl.astype(_bf16)) * mask_ref[l]
        o = jnp.zeros((c, HGRN_W), _f32)
        for h in range(HGRN_HEADS):
            vh = (v * lm_ref[h:h + 1, :]).astype(_bf16)
            o = o + _dot(attn[h * c:(h + 1) * c, :].astype(_bf16), vh)

        st = st_ref[...]
        o = o + _dot_nt((q * e_cum).astype(_bf16), st.astype(_bf16))
        upd = _dot_tn(v_b, (k * e_rest).astype(_bf16)) * bd
        st_ref[...] = st * e_cum[c - 1:c, :] + upd

        osq = o * o
        hi = osq.astype(_bf16)
        lo = (osq - hi.astype(_f32)).astype(_bf16)
        bd_b = bd.astype(_bf16)
        ms = (_dot(hi, bd_b) + _dot(lo, bd_b)) * (1.0 / HGRN_VDIM)
        o = o * lax.rsqrt(ms + RMS_EPS)
        o_ref[rows, :] = o * ng_ref[...] * jax.nn.sigmoid(g_ref[rows, :])

    for ci in range(q_ref.shape[0] // c):
        chunk(ci)


def hgrn2(h, batch, seq, lb, norm_g):
    lmat, mask, lane_mask, bd = _hgrn_constants()
    tt = HGRN_TILE
    nt = seq // tt

    def col(cb):
        return pl.BlockSpec((tt, HGRN_W), lambda b, t: (b * nt + t, cb))

    def const(shape):
        nd = len(shape)
        return pl.BlockSpec(shape, lambda b, t: (0,) * nd)

    return pl.pallas_call(
        _hgrn_kernel,
        out_shape=jax.ShapeDtypeStruct((batch * seq, HGRN_W), _f32),
        grid=(batch, nt),
        in_specs=[col(COL_QH // HGRN_W), col(COL_FH // HGRN_W), col(COL_IH // HGRN_W), col(COL_GH // HGRN_W),
                  const((1, HGRN_W)), const((1, HGRN_W)),
                  const(lmat.shape), const(mask.shape), const(lane_mask.shape), const(bd.shape)],
        out_specs=pl.BlockSpec((tt, HGRN_W), lambda b, t: (b * nt + t, 0)),
        scratch_shapes=[pltpu.VMEM((HGRN_W, HGRN_W), _f32)],
        compiler_params=_cparams(("parallel", "arbitrary")),
        name="hgrn2",
    )(h, h, h, h, lb.reshape(1, HGRN_W), norm_g.reshape(1, HGRN_W),
      jnp.asarray(lmat, _bf16), jnp.asarray(mask), jnp.asarray(lane_mask), jnp.asarray(bd))


def _rel_bucket_np(dist):
    n = np.maximum(dist, 0)
    nf = np.maximum(n, 1).astype(np.float32)
    large = REL_MAX_EXACT + (np.log(nf / np.float32(REL_MAX_EXACT)) / np.float32(math.log(REL_MAX_DIST / REL_MAX_EXACT))
                             * np.float32(REL_BUCKETS - REL_MAX_EXACT)).astype(np.int32)
    return np.where(n < REL_MAX_EXACT, n, np.minimum(large, REL_BUCKETS - 1)).astype(np.int32)


FAR_DIST = 113
assert int(_rel_bucket_np(np.arange(FAR_DIST, 4 * FAR_DIST)).min()) == REL_BUCKETS - 1


LOG2E = 1.4426950408889634
M_FLOOR = -5e29


def _bias_rel(tab, dist):
    dist = np.asarray(dist)
    vals = jnp.take(tab, jnp.asarray(_rel_bucket_np(dist).reshape(-1)), axis=1)
    vals = (vals - tab[:, REL_BUCKETS - 1:]) * LOG2E
    vals = vals.reshape((tab.shape[0],) + dist.shape)
    return jnp.where(jnp.asarray(dist >= 0)[None], vals, NEG)


def _toeplitz_kernel(g_ref, o_ref):
    rows, width = o_ref.shape
    x = jnp.broadcast_to(g_ref[...], (rows, g_ref.shape[1]))
    o_ref[...] = pltpu.roll(x, 0, 1, stride=1, stride_axis=0)[:, :width]


def _toeplitz(g, rows, width):
    nh, length = g.shape
    return pl.pallas_call(
        _toeplitz_kernel,
        out_shape=jax.ShapeDtypeStruct((nh, rows, width), _f32),
        grid=(nh,),
        in_specs=[pl.BlockSpec((None, 1, length), lambda h: (h, 0, 0))],
        out_specs=pl.BlockSpec((None, rows, width), lambda h: (h, 0, 0)),
        compiler_params=_cparams(("parallel",)),
        name="toeplitz_bias",
    )(g.reshape(nh, 1, length))


def _near_bias(tab):
    k = np.arange(2 * (PREV + TQ))
    g = _bias_rel(tab, np.where(k < PREV + TQ, PREV - k, 4 * FAR_DIST))
    return _toeplitz(g, TQ, PREV + TQ)


FAR_TILE = 512
FAR_UNROLL = 4


def _softmax_update(m, acc, s, rhs_b):
    m_new = jnp.maximum(m, jnp.max(s, axis=-1, keepdims=True))
    alpha = jnp.exp2(m - m_new)
    p = jnp.exp2(s - m_new)
    return m_new, alpha * acc + _dot(p.astype(_bf16), rhs_b)


def _far_sweep(n_tiles, tile_step, init, exact_tail):
    def body(it, carry):
        for u in range(FAR_UNROLL):
            carry = tile_step(it * FAR_UNROLL + u, carry)
        return carry

    if not exact_tail:
        return lax.fori_loop(0, (n_tiles + FAR_UNROLL - 1) // FAR_UNROLL, body, init)
    n_main = n_tiles // FAR_UNROLL
    carry = lax.fori_loop(0, n_main, body, init)
    return lax.fori_loop(n_main * FAR_UNROLL, n_tiles, tile_step, carry)


def _block_onehot_neg(seq, block):
    k = lax.broadcasted_iota(jnp.int32, (seq, 128), 0)
    lane = lax.broadcasted_iota(jnp.int32, (seq, 128), 1)
    return jnp.where(lane == k // block, NEG, 0.0).astype(_bf16)


MOBA_UNIT = PREV


def _moba_kernel(q_ref, k_ref, v_ref, et_ref, bn_ref, o_ref, kmean_ref):
    c = pl.program_id(2)
    seq = k_ref.shape[0]
    nb = seq // MOBA_BLOCK
    tq = TQ

    @pl.when(c == 0)
    def _():
        kmean_ref[...] = jnp.sum(k_ref[...].astype(_f32).reshape(nb, MOBA_BLOCK, 128), axis=1) * (1.0 / MOBA_BLOCK)

    s0 = c * tq
    own = s0 // MOBA_BLOCK
    far_end = s0 - PREV
    n_far = (jnp.maximum(far_end, 0) + FAR_TILE - 1) // FAR_TILE

    lane = lax.broadcasted_iota(jnp.int32, (1, 128), 1)
    jcol = lax.broadcasted_iota(jnp.int32, (nb, tq), 0)
    rep = (lax.broadcasted_iota(jnp.int32, (nb, 128), 1) // (MOBA_BLOCK // MOBA_UNIT)
           == lax.broadcasted_iota(jnp.int32, (nb, 128), 0)).astype(_bf16)

    q_all = q_ref[...] * (HEAD_DIM ** -0.5 * LOG2E)
    kmean_b = kmean_ref[...].astype(_bf16)
    q_far, q_near = [], []
    for hh in range(2):
        qh = jnp.where(lane // HEAD_DIM == hh, q_all, 0.0).astype(_bf16)
        gate = _dot_nt(kmean_b, qh)
        work = jnp.where(jcol < own, gate, -BIG)
        sel = jnp.zeros((nb, tq), _f32)
        for _ in range(MOBA_TOPK):
            mx = jnp.max(work, axis=0, keepdims=True)
            idx = jnp.min(jnp.where(work == mx, jcol, nb), axis=0, keepdims=True)
            pick = jcol == idx
            sel = jnp.where(pick & (mx > -0.5 * BIG), 1.0, sel)
            work = jnp.where(pick, -jnp.inf, work)
        sel = jnp.where(jcol == own, 1.0, sel)
        notsel = _dot_tn((1.0 - sel).astype(_bf16), rep)
        notsel_far = jnp.where(lane < far_end // MOBA_UNIT, notsel, 1.0)
        q_far.append(jnp.concatenate([qh, notsel_far.astype(_bf16)], axis=1))
        q_near.append(jnp.concatenate([qh, notsel.astype(_bf16)], axis=1))
    qf = jnp.concatenate(q_far, axis=0)
    qn = jnp.concatenate(q_near, axis=0)
    ones = jnp.ones((FAR_TILE, 128), _bf16)

    def far_tile(kt, carry):
        rows = pl.ds(pl.multiple_of(kt * FAR_TILE, FAR_TILE), FAR_TILE)
        s = _dot_nt(qf, jnp.concatenate([k_ref[rows, :], et_ref[rows, :]], axis=1))
        return _softmax_update(*carry, s, jnp.concatenate([v_ref[rows, :], ones], axis=1))

    init = (jnp.full((2 * tq, 1), M_FLOOR, _f32), jnp.zeros((2 * tq, 256), _f32))
    m, acc = _far_sweep(n_far, far_tile, init, exact_tail=False)

    prev = pl.ds(pl.multiple_of(jnp.maximum(far_end, 0), PREV), PREV)
    diag = pl.ds(pl.multiple_of(s0, tq), tq)
    k_n = jnp.concatenate([jnp.concatenate([k_ref[prev, :], et_ref[prev, :]], axis=1),
                           jnp.concatenate([k_ref[diag, :], et_ref[diag, :]], axis=1)], axis=0)
    v_n = jnp.concatenate([jnp.concatenate([v_ref[prev, :], v_ref[diag, :]], axis=0), ones[:PREV + tq]], axis=1)
    col = lax.broadcasted_iota(jnp.int32, (1, 1, PREV + tq), 2)
    nbias = jnp.where((col < PREV) & (c == 0), NEG, bn_ref[...])
    s = _dot_nt(qn, k_n) + nbias.reshape(2 * tq, PREV + tq)
    _, acc = _softmax_update(m, acc, s, v_n)
    out = acc[:, :128] / acc[:, 128:]
    o_ref[...] = jnp.where(lane // HEAD_DIM == 0, out[:tq], out[tq:])


def moba_attention(h, hb, batch, seq, bias_near):
    nq = seq // TQ
    assert seq // MOBA_UNIT <= 128 and MOBA_BLOCK % TQ == 0 and FAR_TILE >= PREV + TQ
    et = _block_onehot_neg(seq, MOBA_UNIT)

    def kv(col):
        return pl.BlockSpec((seq, 128), lambda b, hp, c: (b, col // 128 + hp))

    return pl.pallas_call(
        _moba_kernel,
        out_shape=jax.ShapeDtypeStruct((batch * seq, MOBA_W), _f32),
        grid=(batch, MOBA_HEADS // 2, nq),
        in_specs=[pl.BlockSpec((TQ, 128), lambda b, hp, c: (b * nq + c, COL_QM // 128 + hp)),
                  kv(COL_KM), kv(COL_VM),
                  pl.BlockSpec((seq, 128), lambda b, hp, c: (0, 0)),
                  pl.BlockSpec((2, TQ, PREV + TQ), lambda b, hp, c: (hp, 0, 0))],
        out_specs=pl.BlockSpec((TQ, 128), lambda b, hp, c: (b * nq + c, hp)),
        scratch_shapes=[pltpu.VMEM((seq // MOBA_BLOCK, 128), _f32)],
        compiler_params=_cparams(("parallel", "parallel", "arbitrary")),
        name="moba_attention",
    )(h, hb, hb, et, bias_near)


def _compress_kernel(a_ref, pet_ref, peb_ref, wt_ref, wb_ref, b1_ref, w2_ref, o_ref):
    a = a_ref[...]
    nrow = a.shape[0]
    top = _dot((a + pet_ref[...]).astype(_bf16), wt_ref[...])
    bot = _dot((a + peb_ref[...]).astype(_bf16), wb_ref[...])
    pre = top + pltpu.roll(bot, nrow - 1, 0) + b1_ref[...]
    hid = jax.nn.gelu(pre)
    o_ref[...] = _dot(hid.astype(_bf16), w2_ref[...])


def nsa_compress(h, batch, seq, pe, w1, b1, w2):
    nr = seq // NSA_CMP_STRIDE
    half = NSA_CMP_LEN // 2
    a = h[:, COL_KVC:COL_KVC + 128].reshape(batch, nr, NSA_CMP_STRIDE * 128)
    pe_cat = jnp.concatenate([pe[0], pe[1]], axis=-1)
    pe_top = pe_cat[:half].reshape(1, half * 128)
    pe_bot = pe_cat[half:].reshape(1, half * 128)
    hid = NSA_CMP_HIDDEN
    w_all = jnp.zeros((NSA_CMP_LEN, 128, 2 * hid), _f32)
    w_all = w_all.at[:, :HEAD_DIM, :hid].set(w1[0].reshape(NSA_CMP_LEN, HEAD_DIM, hid))
    w_all = w_all.at[:, HEAD_DIM:, hid:].set(w1[1].reshape(NSA_CMP_LEN, HEAD_DIM, hid))
    w_top = w_all[:half].reshape(half * 128, 2 * hid).astype(_bf16)
    w_bot = w_all[half:].reshape(half * 128, 2 * hid).astype(_bf16)
    b1cat = jnp.concatenate([b1[0], b1[1]]).reshape(1, 2 * hid)
    w2bd = jnp.zeros((2 * hid, 128), _f32)
    w2bd = w2bd.at[:hid, :HEAD_DIM].set(w2[0]).at[hid:, HEAD_DIM:].set(w2[1]).astype(_bf16)

    def const(shape):
        return pl.BlockSpec(shape, lambda b: (0,) * len(shape))

    return pl.pallas_call(
        _compress_kernel,
        out_shape=jax.ShapeDtypeStruct((batch, nr, 128), _f32),
        grid=(batch,),
        in_specs=[pl.BlockSpec((None, nr, half * 128), lambda b: (b, 0, 0)),
                  const(pe_top.shape), const(pe_bot.shape), const(w_top.shape), const(w_bot.shape),
                  const(b1cat.shape), const(w2bd.shape)],
        out_specs=pl.BlockSpec((None, nr, 128), lambda b: (b, 0, 0)),
        compiler_params=_cparams(("parallel",)),
        name="nsa_compress",
    )(a, pe_top, pe_bot, w_top, w_bot, b1cat, w2bd)


CMP_FIRST = PREV // NSA_CMP_STRIDE
CMP_NEAR = -(-((TQ + PREV - NSA_CMP_LEN) // NSA_CMP_STRIDE + 1) // 8) * 8
NSA_WKEYS = NSA_WINDOW + TQ
NSA_WTILES = NSA_WKEYS // PREV


def _nsa_tables(tab, seq):
    nc = seq // NSA_CMP_STRIDE
    nsel = seq // NSA_SEL_BLOCK
    n_cmp = (seq - NSA_CMP_LEN) // NSA_CMP_STRIDE + 1
    i = np.arange(TQ)[:, None]
    m = np.arange(CMP_NEAR)[None, :]
    d_c = i + (PREV - NSA_CMP_LEN + 1) - NSA_CMP_STRIDE * m
    cmp_bias = _bias_rel(tab, np.maximum(d_c, 0))
    k = np.arange(2 * NSA_WKEYS)
    g = _bias_rel(tab, np.where((k >= 1) & (k <= NSA_WINDOW), NSA_WINDOW - k, -1))
    win_bias = _toeplitz(g, TQ, NSA_WKEYS)
    n = np.arange(nc)[:, None]
    js = np.arange(nsel)[None, :]
    cs, ce = n * NSA_CMP_STRIDE, n * NSA_CMP_STRIDE + NSA_CMP_LEN
    ss = js * NSA_SEL_BLOCK
    ov = np.clip(np.minimum(ce, ss + NSA_SEL_BLOCK) - np.maximum(cs, ss), 0, None).astype(np.float32) / NSA_CMP_LEN
    ov = np.where(n < n_cmp, ov, 0.0)
    return cmp_bias, win_bias, _near_bias(tab), jnp.asarray(ov.T, _bf16)


def _exact_dot_r(x, r_bf16):
    hi = x.astype(_bf16)
    lo = (x - hi.astype(_f32)).astype(_bf16)
    return _dot(hi, r_bf16) + _dot(lo, r_bf16)


def _nsa_kernel(q_ref, gn_ref, kvc_ref, kvs_ref, kvw_ref, et_ref, cb_ref, wb_ref, nb_ref, ovt_ref, o_ref):
    c = pl.program_id(1)
    tq = TQ
    nh = NSA_HEADS
    nc = kvc_ref.shape[0]
    nsel = ovt_ref.shape[0]
    s0 = c * tq
    far_end = s0 - PREV
    lane = lax.broadcasted_iota(jnp.int32, (1, 128), 1)
    low = lane < HEAD_DIM

    q_all = q_ref[...] * (HEAD_DIM ** -0.5 * LOG2E)
    qh = []
    for h in range(nh):
        blk = q_all[:, 128 * (h // 2):128 * (h // 2) + 128]
        if h % 2 == 1:
            blk = pltpu.roll(blk, HEAD_DIM, 1)
        qh.append(jnp.where(low, blk, 0.0))
    q_b = jnp.concatenate(qh, axis=0).astype(_bf16)

    kvc_b = kvc_ref[...].astype(_bf16)
    s_c = _dot_nt(q_b, kvc_b).reshape(nh, tq, nc)
    n_idx = lax.broadcasted_iota(jnp.int32, (tq, nc), 1)
    t_idx = lax.broadcasted_iota(jnp.int32, (tq, nc), 0) + s0
    mask_c = jnp.where((n_idx * NSA_CMP_STRIDE + (NSA_CMP_LEN - 1)) <= t_idx, 0.0, NEG)
    m_sel = lax.broadcasted_iota(jnp.int32, (CMP_NEAR, nc), 0)
    n_sel = lax.broadcasted_iota(jnp.int32, (CMP_NEAR, nc), 1)
    place = (n_sel == (s0 // NSA_CMP_STRIDE) - CMP_FIRST + m_sel).astype(_bf16)
    bias_c = _exact_dot_r(cb_ref[...].reshape(nh * tq, CMP_NEAR), place).reshape(nh, tq, nc)
    psum = jnp.zeros((tq, nc), _f32)
    o_c = []
    for h in range(nh):
        s = s_c[h] + bias_c[h] + mask_c
        mx = jnp.maximum(jnp.max(s, axis=1, keepdims=True), M_FLOOR)
        e = jnp.exp2(s - mx)
        p = e * (1.0 / jnp.maximum(jnp.sum(e, axis=1, keepdims=True), 1e-30))
        psum = psum + p
        o_c.append(_dot(p.astype(_bf16), kvc_b))

    imp_t = jnp.zeros((nsel, tq), _f32)
    for part in _split3(psum):
        imp_t = imp_t + _dot_nt(ovt_ref[...], part)
    jblk = lax.broadcasted_iota(jnp.int32, (nsel, tq), 0)
    blk_t = (lax.broadcasted_iota(jnp.int32, (nsel, tq), 1) + s0) // NSA_SEL_BLOCK
    forced = (jblk == 0) | (jblk == blk_t) | (jblk == blk_t - 1)
    work = jnp.where(forced, BIG, jnp.where(jblk <= blk_t, imp_t, -BIG))
    sel = jnp.zeros((nsel, tq), _f32)
    for _ in range(min(NSA_TOPN, nsel)):
        mx = jnp.max(work, axis=0, keepdims=True)
        idx = jnp.min(jnp.where(work == mx, jblk, nsel), axis=0, keepdims=True)
        pick = jblk == idx
        sel = jnp.where(pick, 1.0, sel)
        work = jnp.where(pick, -jnp.inf, work)

    eye = (lax.broadcasted_iota(jnp.int32, (nsel, 128), 0)
           == lax.broadcasted_iota(jnp.int32, (nsel, 128), 1)).astype(_bf16)
    notsel = _dot_tn((1.0 - sel).astype(_bf16), eye)
    notsel_far = jnp.where(lane < far_end // NSA_SEL_BLOCK, notsel, 1.0)
    qf = jnp.concatenate([q_b, jnp.concatenate([notsel_far.astype(_bf16)] * nh, axis=0)], axis=1)
    qn = jnp.concatenate([q_b, jnp.concatenate([notsel.astype(_bf16)] * nh, axis=0)], axis=1)

    def ones_v(kv):
        return jnp.where(low, 1.0, kv).astype(_bf16)

    def far_tile(kt, carry):
        rows = pl.ds(pl.multiple_of(kt * FAR_TILE, FAR_TILE), FAR_TILE)
        kv = kvs_ref[rows, :]
        s = _dot_nt(qf, jnp.concatenate([kv, et_ref[rows, :]], axis=1))
        return _softmax_update(*carry, s, ones_v(kv))

    n_far = (jnp.maximum(far_end, 0) + FAR_TILE - 1) // FAR_TILE
    init = (jnp.full((nh * tq, 1), M_FLOOR, _f32), jnp.zeros((nh * tq, 128), _f32))
    m, acc = _far_sweep(n_far, far_tile, init, exact_tail=True)

    prev = pl.ds(pl.multiple_of(jnp.maximum(far_end, 0), PREV), PREV)
    diag = pl.ds(pl.multiple_of(s0, tq), tq)
    kv = jnp.concatenate([kvs_ref[prev, :], kvs_ref[diag, :]], axis=0)
    et = jnp.concatenate([et_ref[prev, :], et_ref[diag, :]], axis=0)
    col = lax.broadcasted_iota(jnp.int32, (1, 1, PREV + tq), 2)
    nbias = jnp.where((col < PREV) & (c == 0), NEG, nb_ref[...])
    s = _dot_nt(qn, jnp.concatenate([kv, et], axis=1)) + nbias.reshape(nh * tq, PREV + tq)
    _, acc = _softmax_update(m, acc, s, ones_v(kv))
    o_s = (acc / acc[:, 0:1]).reshape(nh, tq, 128)

    tiles = []
    for r in range(NSA_WTILES):
        start = pl.multiple_of(jnp.maximum(s0 - NSA_WINDOW + r * PREV, 0), PREV)
        tiles.append(kvw_ref[pl.ds(start, PREV), :])
    kvw_b = jnp.concatenate(tiles, axis=0)
    wk = NSA_WKEYS
    kpos_w = lax.broadcasted_iota(jnp.int32, (1, 1, wk), 2) + (s0 - NSA_WINDOW)
    s = _dot_nt(q_b, kvw_b).reshape(nh, tq, wk) + jnp.where(kpos_w >= 0, wb_ref[...], NEG)
    e = jnp.exp2(s - jnp.max(s, axis=2, keepdims=True))
    acc_w = _dot(e.reshape(nh * tq, wk).astype(_bf16), ones_v(kvw_b))
    o_w = (acc_w / acc_w[:, 0:1]).reshape(nh, tq, 128)

    g = jax.nn.sigmoid(gn_ref[...])
    outs = []
    for h in range(nh):
        outs.append(g[:, 3 * h:3 * h + 1] * o_c[h] + g[:, 3 * h + 1:3 * h + 2] * o_s[h]
                    + g[:, 3 * h + 2:3 * h + 3] * o_w[h])
    for hp in range(nh // 2):
        o_ref[:, 128 * hp:128 * hp + 128] = jnp.where(low, pltpu.roll(outs[2 * hp], HEAD_DIM, 1), outs[2 * hp + 1])


def nsa_attention(h, hb, kvc, batch, seq, tables):
    nq = seq // TQ
    nc = seq // NSA_CMP_STRIDE
    cmp_bias, win_bias, near_bias, ov_t = tables
    assert seq // NSA_SEL_BLOCK <= 128
    et = _block_onehot_neg(seq, NSA_SEL_BLOCK)

    def const(shape):
        return pl.BlockSpec(shape, lambda b, c: (0,) * len(shape))

    return pl.pallas_call(
        _nsa_kernel,
        out_shape=jax.ShapeDtypeStruct((batch * seq, NSA_W), _f32),
        grid=(batch, nq),
        in_specs=[pl.BlockSpec((TQ, NSA_W), lambda b, c: (b * nq + c, COL_QN // NSA_W)),
                  pl.BlockSpec((TQ, 128), lambda b, c: (b * nq + c, COL_GN // 128)),
                  pl.BlockSpec((None, nc, 128), lambda b, c: (b, 0, 0)),
                  pl.BlockSpec((seq, 128), lambda b, c: (b, COL_KVS // 128)),
                  pl.BlockSpec((seq, 128), lambda b, c: (b, COL_KVW // 128)),
                  const(et.shape),
                  const(cmp_bias.shape), const(win_bias.shape), const(near_bias.shape), const(ov_t.shape)],
        out_specs=pl.BlockSpec((TQ, NSA_W), lambda b, c: (b * nq + c, 0)),
        compiler_params=_cparams(("parallel", "arbitrary")),
        name="nsa_attention",
    )(h, h, kvc, hb, hb, et, cmp_bias, win_bias, near_bias, ov_t)


def kernel(x, w_in, nsa_cmp_pe, nsa_cmp_w1, nsa_cmp_b1, nsa_cmp_w2, conv_dw_w, conv_dw_b, conv_ln_g,
           conv_ln_b, conv_pw_w, hgrn_lb_logits, hgrn_norm_g, w_out, ln1_g, ln1_b, w_ff1, w_ff2,
           ln2_g, ln2_b, rel_bias):
    batch, seq, d = x.shape
    lb_sm = jax.nn.softmax(hgrn_lb_logits.astype(_f32), axis=0)
    lbs = jnp.cumsum(lb_sm, axis=0) - lb_sm[0]
    x2 = x.reshape(batch * seq, d)
    nsa_tables = _nsa_tables(rel_bias[:NSA_HEADS], seq)
    moba_near = _near_bias(rel_bias[NSA_HEADS:])
    for l in range(DEPTH):
        h, hb = in_proj(x2, _regroup_w_in(w_in[l]))
        o_conv = conformer_conv(h, batch, seq, conv_dw_w[l], conv_dw_b[l], conv_ln_g[l], conv_ln_b[l],
                                conv_pw_w[l].astype(_bf16))
        o_hgrn = hgrn2(h, batch, seq, lbs[l], hgrn_norm_g[l])
        kvc = nsa_compress(h, batch, seq, nsa_cmp_pe[l], nsa_cmp_w1[l], nsa_cmp_b1[l], nsa_cmp_w2[l])
        o_nsa = nsa_attention(h, hb, kvc, batch, seq, nsa_tables)
        o_moba = moba_attention(h, hb, batch, seq, moba_near)
        x1 = out_proj_ln(o_nsa, o_moba, o_conv, o_hgrn, x2, w_out[l].astype(_bf16), ln1_g[l], ln1_b[l])
        x2 = mlp_ln(x1, w_ff1[l].astype(_bf16), w_ff2[l].astype(_bf16), ln2_g[l], ln2_b[l])
    return x2.reshape(batch, seq, d)
```

```python
import math

import jax
import jax.numpy as jnp
import numpy as np
from jax import lax
from jax.experimental import pallas as pl
from jax.experimental.pallas import tpu as pltpu

D_MODEL = 1024
DEPTH = 2
HEAD_DIM = 64
NSA_HEADS = 4
NSA_W = NSA_HEADS * HEAD_DIM
NSA_CMP_LEN = 32
NSA_CMP_STRIDE = 16
NSA_CMP_HIDDEN = 256
NSA_SEL_BLOCK = 64
NSA_TOPN = 16
NSA_WINDOW = 512
MOBA_HEADS = 4
MOBA_W = MOBA_HEADS * HEAD_DIM
MOBA_BLOCK = 256
MOBA_TOPK = 3
CONV_CH = 256
CONV_WIDTH = 31
HGRN_HEADS = 4
HGRN_KDIM = 64
HGRN_VDIM = 64
HGRN_W = HGRN_HEADS * HGRN_VDIM
HGRN_CHUNK = 64
MIX_WIDTH = NSA_W + MOBA_W + CONV_CH + HGRN_W
IN_WIDTHS = (NSA_W, HEAD_DIM, HEAD_DIM, HEAD_DIM, HEAD_DIM, HEAD_DIM, HEAD_DIM, 3 * NSA_HEADS,
             MOBA_W, MOBA_W, MOBA_W, 2 * CONV_CH, HGRN_W, HGRN_W, HGRN_W, HGRN_W)
D_FF = 4 * D_MODEL
REL_BUCKETS = 32
REL_MAX_EXACT = 16
REL_MAX_DIST = 128
TQ = 256
MOBA_TQ = 512
PAIR = 2
PREV = 128
LN_EPS = 1e-5
RMS_EPS = 1e-6
BIG = 1e9
DEEPNORM_ALPHA = (2 * DEPTH) ** 0.25

H_COLS = 3072
COL_QN = 0
COL_KVC = 256
COL_KVS = 384
COL_KVW = 512
COL_GN = 640
COL_QM = 768
COL_KM = 1024
COL_VM = 1280
COL_CONV = 1536
COL_QH = 2048
COL_FH = 2304
COL_IH = 2560
COL_GH = 2816

VMEM_LIMIT = 48 * 1024 * 1024
NEG = -1e30

_f32 = jnp.float32
_bf16 = jnp.bfloat16


def _cparams(sem):
    return pltpu.CompilerParams(dimension_semantics=sem, vmem_limit_bytes=VMEM_LIMIT)


def _layer_norm(z, g, b):
    mu = jnp.mean(z, axis=-1, keepdims=True)
    zc = z - mu
    var = jnp.mean(zc * zc, axis=-1, keepdims=True)
    return zc * lax.rsqrt(var + LN_EPS) * g + b


def _in_proj_kernel(x_ref, w_ref, o_ref, ob_ref):
    acc = jnp.dot(x_ref[...].astype(_bf16), w_ref[...], preferred_element_type=_f32)
    o_ref[...] = acc
    ob_ref[...] = acc.astype(_bf16)


def in_proj(x2, w_bf16, tm=512):
    n, d = x2.shape
    cols = w_bf16.shape[1]
    out = pl.BlockSpec((tm, cols), lambda i: (i, 0))
    return pl.pallas_call(
        _in_proj_kernel,
        out_shape=(jax.ShapeDtypeStruct((n, cols), _f32), jax.ShapeDtypeStruct((n, cols), _bf16)),
        grid=(n // tm,),
        in_specs=[pl.BlockSpec((tm, d), lambda i: (i, 0)),
                  pl.BlockSpec((d, cols), lambda i: (0, 0))],
        out_specs=(out, out),
        compiler_params=_cparams(("parallel",)),
        name="in_proj",
    )(x2, w_bf16)


def _regroup_w_in(w_in_l):
    gate_end = COL_GN + 3 * NSA_HEADS
    src = lax.broadcasted_iota(jnp.int32, (gate_end + H_COLS - COL_QM, H_COLS), 0)
    dst = lax.broadcasted_iota(jnp.int32, (gate_end + H_COLS - COL_QM, H_COLS), 1)
    place = (dst == jnp.where(src < gate_end, src, src + (COL_QM - gate_end))).astype(_bf16)
    assert place.shape[0] == w_in_l.shape[1] == sum(IN_WIDTHS)
    return jnp.dot(w_in_l.astype(_bf16), place, preferred_element_type=_f32).astype(_bf16)


def _out_proj_kernel(a_ref, b_ref, c_ref, d_ref, x_ref, w_ref, g_ref, beta_ref, o_ref):
    mixed = jnp.dot(a_ref[...].astype(_bf16), w_ref[0:256, :], preferred_element_type=_f32)
    mixed += jnp.dot(b_ref[...].astype(_bf16), w_ref[256:512, :], preferred_element_type=_f32)
    mixed += jnp.dot(c_ref[...].astype(_bf16), w_ref[512:768, :], preferred_element_type=_f32)
    mixed += jnp.dot(d_ref[...].astype(_bf16), w_ref[768:1024, :], preferred_element_type=_f32)
    z = DEEPNORM_ALPHA * x_ref[...] + mixed
    o_ref[...] = _layer_norm(z, g_ref[...], beta_ref[...])


def out_proj_ln(o_nsa, o_moba, o_conv, o_hgrn, x2, w_out_bf16, g, b, tm=512):
    n, d = x2.shape
    part = pl.BlockSpec((tm, 256), lambda i: (i, 0))
    vec = pl.BlockSpec((1, d), lambda i: (0, 0))
    return pl.pallas_call(
        _out_proj_kernel,
        out_shape=jax.ShapeDtypeStruct((n, d), _f32),
        grid=(n // tm,),
        in_specs=[part, part, part, part,
                  pl.BlockSpec((tm, d), lambda i: (i, 0)),
                  pl.BlockSpec((MIX_WIDTH, d), lambda i: (0, 0)),
                  vec, vec],
        out_specs=pl.BlockSpec((tm, d), lambda i: (i, 0)),
        compiler_params=_cparams(("parallel",)),
        name="out_proj_ln",
    )(o_nsa, o_moba, o_conv, o_hgrn, x2, w_out_bf16, g.reshape(1, d), b.reshape(1, d))


def _mlp_kernel(x_ref, w1_ref, w2_ref, g_ref, beta_ref, o_ref, acc_ref):
    j = pl.program_id(1)

    @pl.when(j == 0)
    def _():
        acc_ref[...] = jnp.zeros_like(acc_ref)

    hid = jnp.dot(x_ref[...].astype(_bf16), w1_ref[...], preferred_element_type=_f32)
    hid = jnp.square(jnp.maximum(hid, 0.0))
    acc_ref[...] += jnp.dot(hid.astype(_bf16), w2_ref[...], preferred_element_type=_f32)

    @pl.when(j == pl.num_programs(1) - 1)
    def _():
        z = DEEPNORM_ALPHA * x_ref[...] + acc_ref[...]
        o_ref[...] = _layer_norm(z, g_ref[...], beta_ref[...])


def mlp_ln(x1, w1_bf16, w2_bf16, g, b, tm=1024, tf=1024):
    n, d = x1.shape
    dff = w1_bf16.shape[1]
    vec = pl.BlockSpec((1, d), lambda i, j: (0, 0))
    return pl.pallas_call(
        _mlp_kernel,
        out_shape=jax.ShapeDtypeStruct((n, d), _f32),
        grid=(n // tm, dff // tf),
        in_specs=[pl.BlockSpec((tm, d), lambda i, j: (i, 0)),
                  pl.BlockSpec((d, tf), lambda i, j: (0, j)),
                  pl.BlockSpec((tf, d), lambda i, j: (j, 0)),
                  vec, vec],
        out_specs=pl.BlockSpec((tm, d), lambda i, j: (i, 0)),
        scratch_shapes=[pltpu.VMEM((tm, d), _f32)],
        compiler_params=_cparams(("parallel", "arbitrary")),
        name="mlp_ln",
    )(x1, w1_bf16, w2_bf16, g.reshape(1, d), b.reshape(1, d))


CONV_HALO = 32


def _conv_kernel(u_ref, wdw_ref, bdw_ref, g_ref, beta_ref, wpw_ref, o_ref, buf_ref):
    t = pl.program_id(1)
    tt = u_ref.shape[0]

    @pl.when(t == 0)
    def _():
        buf_ref[0:CONV_HALO, :] = jnp.zeros((CONV_HALO, CONV_CH), _f32)

    @pl.when(t > 0)
    def _():
        buf_ref[0:CONV_HALO, :] = buf_ref[tt:tt + CONV_HALO, :]

    u = u_ref[...]
    a = u[:, :CONV_CH]
    gl = u[:, CONV_CH:]
    buf_ref[CONV_HALO:CONV_HALO + tt, :] = a * jax.nn.sigmoid(gl)

    acc = jnp.zeros((tt, CONV_CH), _f32) + bdw_ref[...]
    for w in range(CONV_WIDTH):
        off = CONV_HALO - (CONV_WIDTH - 1) + w
        acc = acc + buf_ref[off:off + tt, :] * wdw_ref[w:w + 1, :]
    hn = _layer_norm(acc, g_ref[...], beta_ref[...])
    hn = hn * jax.nn.sigmoid(hn)
    o_ref[...] = jnp.dot(hn.astype(_bf16), wpw_ref[...], preferred_element_type=_f32)


def conformer_conv(h, batch, seq, w_dw, b_dw, ln_g, ln_b, w_pw_bf16, tt=512):
    nt = seq // tt
    vec = pl.BlockSpec((1, CONV_CH), lambda b, t: (0, 0))
    return pl.pallas_call(
        _conv_kernel,
        out_shape=jax.ShapeDtypeStruct((batch * seq, CONV_CH), _f32),
        grid=(batch, nt),
        in_specs=[pl.BlockSpec((tt, 2 * CONV_CH), lambda b, t: (b * nt + t, COL_CONV // (2 * CONV_CH))),
                  pl.BlockSpec((CONV_WIDTH, CONV_CH), lambda b, t: (0, 0)),
                  vec, vec, vec,
                  pl.BlockSpec((CONV_CH, CONV_CH), lambda b, t: (0, 0))],
        out_specs=pl.BlockSpec((tt, CONV_CH), lambda b, t: (b * nt + t, 0)),
        scratch_shapes=[pltpu.VMEM((CONV_HALO + tt, CONV_CH), _f32)],
        compiler_params=_cparams(("parallel", "arbitrary")),
        name="conformer_conv",
    )(h, w_dw, b_dw.reshape(1, CONV_CH), ln_g.reshape(1, CONV_CH), ln_b.reshape(1, CONV_CH), w_pw_bf16)


HGRN_LEVELS = (32, 16, 8, 4, 2, 1)
HGRN_TILE = 256


def _hgrn_constants():
    c = HGRN_CHUNK
    t = np.arange(c)[:, None]
    u = np.arange(c)[None, :]
    blocks = [(u <= t), (u > t)]
    masks = []
    for h in HGRN_LEVELS:
        mid = (t // (2 * h)) * (2 * h) + h
        upper = t >= mid
        blocks.append((upper & (u >= mid) & (u <= t)) | ((~upper) & (u > t) & (u <= mid - 1)))
        same = (t // (2 * h)) == (u // (2 * h))
        mid_s = (u // (2 * h)) * (2 * h) + h
        masks.append(same & upper & (u < mid_s))
    masks.append(t == u)
    lmat = np.concatenate(blocks, axis=0).astype(np.float32)
    mask = np.stack(masks).astype(np.float32)
    mask = np.tile(mask, (1, HGRN_HEADS, 1))
    lane_head = np.arange(HGRN_W)[None, :] // HGRN_KDIM
    lane_mask = (lane_head == np.arange(HGRN_HEADS)[:, None]).astype(np.float32)
    bd = (lane_head.T == lane_head).astype(np.float32)
    return lmat, mask, lane_mask, bd


def _split3(x):
    hi = x.astype(_bf16)
    r = x - hi.astype(_f32)
    mid = r.astype(_bf16)
    lo = (r - mid.astype(_f32)).astype(_bf16)
    return hi, mid, lo


def _dot(a, b):
    return jnp.dot(a, b, preferred_element_type=_f32)


def _dot_nt(a, b):
    return lax.dot_general(a, b, (((1,), (1,)), ((), ())), preferred_element_type=_f32)


def _dot_tn(a, b):
    return lax.dot_general(a, b, (((0,), (0,)), ((), ())), preferred_element_type=_f32)


def _exact_dot(l_bf16, x):
    hi = x.astype(_bf16)
    lo = (x - hi.astype(_f32)).astype(_bf16)
    return _dot(l_bf16, hi) + _dot(l_bf16, lo)


def _hgrn_kernel(q_ref, f_ref, i_ref, g_ref, lb_ref, ng_ref, lmat_ref, mask_ref, lm_ref, bd_ref,
                 o_ref, st_ref):
    c = HGRN_CHUNK
    nl = len(HGRN_LEVELS)

    @pl.when(pl.program_id(1) == 0)
    def _():
        st_ref[...] = jnp.zeros_like(st_ref)

    lb = lb_ref[...]
    bd = bd_ref[...]

    def chunk(ci):
        rows = pl.ds(ci * c, c)
        q = q_ref[rows, :]
        f = lb + (1.0 - lb) * jax.nn.sigmoid(f_ref[rows, :])
        logf = jnp.log(f)
        k = 1.0 - f
        v = i_ref[rows, :]
        v_b = v.astype(_bf16)
        ex = jnp.exp(_exact_dot(lmat_ref[...], logf))
        e_cum = ex[0:c]
        e_rest = ex[c:2 * c]

        attn = jnp.zeros((HGRN_HEADS * c, c), _f32)
        for l in range(nl + 1):
            if l < nl:
                el = ex[(2 + l) * c:(3 + l) * c]
                ql, kl = q * el, k * el
            else:
                ql, kl = q, k
            qs = jnp.concatenate([ql * lm_ref[h:h + 1, :] for h in range(HGRN_HEADS)], axis=0)
            attn = attn + _dot_nt(qs.astype(_bf16), kl.astype(_bf16)) * mask_ref[l]
        o = jnp.zeros((c, HGRN_W), _f32)
        for h in range(HGRN_HEADS):
            vh = (v * lm_ref[h:h + 1, :]).astype(_bf16)
            o = o + _dot(attn[h * c:(h + 1) * c, :].astype(_bf16), vh)

        st = st_ref[...]
        o = o + _dot_nt((q * e_cum).astype(_bf16), st.astype(_bf16))
        upd = _dot_tn(v_b, (k * e_rest).astype(_bf16)) * bd
        st_ref[...] = st * e_cum[c - 1:c, :] + upd

        osq = o * o
        hi = osq.astype(_bf16)
        lo = (osq - hi.astype(_f32)).astype(_bf16)
        bd_b = bd.astype(_bf16)
        ms = (_dot(hi, bd_b) + _dot(lo, bd_b)) * (1.0 / HGRN_VDIM)
        o = o * lax.rsqrt(ms + RMS_EPS)
        o_ref[rows, :] = o * ng_ref[...] * jax.nn.sigmoid(g_ref[rows, :])

    for ci in range(q_ref.shape[0] // c):
        chunk(ci)


def hgrn2(h, batch, seq, lb, norm_g):
    lmat, mask, lane_mask, bd = _hgrn_constants()
    tt = HGRN_TILE
    nt = seq // tt

    def col(cb):
        return pl.BlockSpec((tt, HGRN_W), lambda b, t: (b * nt + t, cb))

    def const(shape):
        nd = len(shape)
        return pl.BlockSpec(shape, lambda b, t: (0,) * nd)

    return pl.pallas_call(
        _hgrn_kernel,
        out_shape=jax.ShapeDtypeStruct((batch * seq, HGRN_W), _f32),
        grid=(batch, nt),
        in_specs=[col(COL_QH // HGRN_W), col(COL_FH // HGRN_W), col(COL_IH // HGRN_W), col(COL_GH // HGRN_W),
                  const((1, HGRN_W)), const((1, HGRN_W)),
                  const(lmat.shape), const(mask.shape), const(lane_mask.shape), const(bd.shape)],
        out_specs=pl.BlockSpec((tt, HGRN_W), lambda b, t: (b * nt + t, 0)),
        scratch_shapes=[pltpu.VMEM((HGRN_W, HGRN_W), _f32)],
        compiler_params=_cparams(("parallel", "arbitrary")),
        name="hgrn2",
    )(h, h, h, h, lb.reshape(1, HGRN_W), norm_g.reshape(1, HGRN_W),
      jnp.asarray(lmat, _bf16), jnp.asarray(mask), jnp.asarray(lane_mask), jnp.asarray(bd))


def _rel_bucket_np(dist):
    n = np.maximum(dist, 0)
    nf = np.maximum(n, 1).astype(np.float32)
    large = REL_MAX_EXACT + (np.log(nf / np.float32(REL_MAX_EXACT)) / np.float32(math.log(REL_MAX_DIST / REL_MAX_EXACT))
                             * np.float32(REL_BUCKETS - REL_MAX_EXACT)).astype(np.int32)
    return np.where(n < REL_MAX_EXACT, n, np.minimum(large, REL_BUCKETS - 1)).astype(np.int32)


FAR_DIST = 113
assert int(_rel_bucket_np(np.arange(FAR_DIST, 4 * FAR_DIST)).min()) == REL_BUCKETS - 1


LOG2E = 1.4426950408889634
M_FLOOR = -5e29


def _bias_rel(tab, dist):
    dist = np.asarray(dist)
    vals = jnp.take(tab, jnp.asarray(_rel_bucket_np(dist).reshape(-1)), axis=1)
    vals = (vals - tab[:, REL_BUCKETS - 1:]) * LOG2E
    vals = vals.reshape((tab.shape[0],) + dist.shape)
    return jnp.where(jnp.asarray(dist >= 0)[None], vals, NEG)


def _toeplitz_kernel(g_ref, o_ref):
    rows, width = o_ref.shape
    x = jnp.broadcast_to(g_ref[...], (rows, g_ref.shape[1]))
    o_ref[...] = pltpu.roll(x, 0, 1, stride=1, stride_axis=0)[:, :width]


def _toeplitz(g, rows, width):
    nh, length = g.shape
    return pl.pallas_call(
        _toeplitz_kernel,
        out_shape=jax.ShapeDtypeStruct((nh, rows, width), _f32),
        grid=(nh,),
        in_specs=[pl.BlockSpec((None, 1, length), lambda h: (h, 0, 0))],
        out_specs=pl.BlockSpec((None, rows, width), lambda h: (h, 0, 0)),
        compiler_params=_cparams(("parallel",)),
        name="toeplitz_bias",
    )(g.reshape(nh, 1, length))


def _near_bias(tab, tq):
    k = np.arange(2 * (PREV + tq))
    g = _bias_rel(tab, np.where(k < PREV + tq, PREV - k, 4 * FAR_DIST))
    return _toeplitz(g, tq, PREV + tq)


FAR_TILE = 512
FAR_UNROLL = 4


def _softmax_update(m, acc, s, rhs_b):
    m_new = jnp.maximum(m, jnp.max(s, axis=-1, keepdims=True))
    alpha = jnp.exp2(m - m_new)
    p = jnp.exp2(s - m_new)
    return m_new, alpha * acc + _dot(p.astype(_bf16), rhs_b)


def _far_sweep(n_tiles, tile_step, init, exact_tail):
    def body(it, carry):
        for u in range(FAR_UNROLL):
            carry = tile_step(it * FAR_UNROLL + u, carry)
        return carry

    if not exact_tail:
        return lax.fori_loop(0, (n_tiles + FAR_UNROLL - 1) // FAR_UNROLL, body, init)
    n_main = n_tiles // FAR_UNROLL
    carry = lax.fori_loop(0, n_main, body, init)
    return lax.fori_loop(n_main * FAR_UNROLL, n_tiles, tile_step, carry)


def _block_onehot_neg(seq, block):
    k = lax.broadcasted_iota(jnp.int32, (seq, 128), 0)
    lane = lax.broadcasted_iota(jnp.int32, (seq, 128), 1)
    return jnp.where(lane == k // block, NEG, 0.0).astype(_bf16)


MOBA_UNIT = PREV


def _moba_kernel(q_ref, k_ref, v_ref, et_ref, bn_ref, o_ref, kmean_ref):
    c = pl.program_id(2)
    seq = k_ref.shape[1]
    nb = seq // MOBA_BLOCK
    tq = MOBA_TQ

    @pl.when(c == 0)
    def _():
        for g in range(PAIR):
            kmean_ref[g] = jnp.sum(k_ref[g].astype(_f32).reshape(nb, MOBA_BLOCK, 128), axis=1) * (1.0 / MOBA_BLOCK)

    s0 = c * tq
    far_end = s0 - PREV
    n_far = (jnp.maximum(far_end, 0) + FAR_TILE - 1) // FAR_TILE

    lane = lax.broadcasted_iota(jnp.int32, (1, 128), 1)
    jcol = lax.broadcasted_iota(jnp.int32, (nb, tq), 0)
    own = (lax.broadcasted_iota(jnp.int32, (nb, tq), 1) + s0) // MOBA_BLOCK
    rep = (lax.broadcasted_iota(jnp.int32, (nb, 128), 1) // (MOBA_BLOCK // MOBA_UNIT)
           == lax.broadcasted_iota(jnp.int32, (nb, 128), 0)).astype(_bf16)

    qfs, qns = [], []
    for g in range(PAIR):
        q_all = q_ref[g] * (HEAD_DIM ** -0.5 * LOG2E)
        kmean_b = kmean_ref[g].astype(_bf16)
        q_far, q_near = [], []
        for hh in range(2):
            qh = jnp.where(lane // HEAD_DIM == hh, q_all, 0.0).astype(_bf16)
            gate = _dot_nt(kmean_b, qh)
            work = jnp.where(jcol < own, gate, -BIG)
            sel = jnp.zeros((nb, tq), _f32)
            for _ in range(MOBA_TOPK):
                mx = jnp.max(work, axis=0, keepdims=True)
                idx = jnp.min(jnp.where(work == mx, jcol, nb), axis=0, keepdims=True)
                pick = jcol == idx
                sel = jnp.where(pick & (mx > -0.5 * BIG), 1.0, sel)
                work = jnp.where(pick, -jnp.inf, work)
            sel = jnp.where(jcol == own, 1.0, sel)
            notsel = _dot_tn((1.0 - sel).astype(_bf16), rep)
            notsel_far = jnp.where(lane < far_end // MOBA_UNIT, notsel, 1.0)
            q_far.append(jnp.concatenate([qh, notsel_far.astype(_bf16)], axis=1))
            q_near.append(jnp.concatenate([qh, notsel.astype(_bf16)], axis=1))
        qfs.append(jnp.concatenate(q_far, axis=0))
        qns.append(jnp.concatenate(q_near, axis=0))
    ones = jnp.ones((max(FAR_TILE, PREV + tq), 128), _bf16)

    def far_tile(kt, carry):
        rows = pl.ds(pl.multiple_of(kt * FAR_TILE, FAR_TILE), FAR_TILE)
        et = et_ref[rows, :]
        out = []
        for g in range(PAIR):
            s = _dot_nt(qfs[g], jnp.concatenate([k_ref[g, rows, :], et], axis=1))
            out.append(_softmax_update(*carry[g], s, jnp.concatenate([v_ref[g, rows, :], ones[:FAR_TILE]], axis=1)))
        return tuple(out)

    init = tuple((jnp.full((2 * tq, 1), M_FLOOR, _f32), jnp.zeros((2 * tq, 256), _f32)) for _ in range(PAIR))
    state = _far_sweep(n_far, far_tile, init, exact_tail=False)

    prev = pl.ds(pl.multiple_of(jnp.maximum(far_end, 0), PREV), PREV)
    diag = pl.ds(pl.multiple_of(s0, tq), tq)
    col = lax.broadcasted_iota(jnp.int32, (1, 1, PREV + tq), 2)
    nbias = jnp.where((col < PREV) & (c == 0), NEG, bn_ref[...])
    nbias = nbias.reshape(2 * tq, PREV + tq)
    et_n = jnp.concatenate([et_ref[prev, :], et_ref[diag, :]], axis=0)
    for g in range(PAIR):
        k_n = jnp.concatenate([jnp.concatenate([k_ref[g, prev, :], k_ref[g, diag, :]], axis=0), et_n], axis=1)
        v_n = jnp.concatenate([jnp.concatenate([v_ref[g, prev, :], v_ref[g, diag, :]], axis=0), ones[:PREV + tq]],
                              axis=1)
        _, acc = _softmax_update(*state[g], _dot_nt(qns[g], k_n) + nbias, v_n)
        out = acc[:, :128] / acc[:, 128:]
        o_ref[g] = jnp.where(lane // HEAD_DIM == 0, out[:tq], out[tq:])


def moba_attention(h, hb, batch, seq, bias_near):
    tq = MOBA_TQ
    nq = seq // tq
    assert seq // MOBA_UNIT <= 128 and tq % MOBA_BLOCK == 0 and batch % PAIR == 0
    et = _block_onehot_neg(seq, MOBA_UNIT)
    h3 = h.reshape(batch, seq, H_COLS)
    hb3 = hb.reshape(batch, seq, H_COLS)

    def kv(col):
        return pl.BlockSpec((PAIR, seq, 128), lambda b, hp, c: (b, 0, col // 128 + hp))

    out = pl.pallas_call(
        _moba_kernel,
        out_shape=jax.ShapeDtypeStruct((batch, seq, MOBA_W), _f32),
        grid=(batch // PAIR, MOBA_HEADS // 2, nq),
        in_specs=[pl.BlockSpec((PAIR, tq, 128), lambda b, hp, c: (b, c, COL_QM // 128 + hp)),
                  kv(COL_KM), kv(COL_VM),
                  pl.BlockSpec((seq, 128), lambda b, hp, c: (0, 0)),
                  pl.BlockSpec((2, tq, PREV + tq), lambda b, hp, c: (hp, 0, 0))],
        out_specs=pl.BlockSpec((PAIR, tq, 128), lambda b, hp, c: (b, c, hp)),
        scratch_shapes=[pltpu.VMEM((PAIR, seq // MOBA_BLOCK, 128), _f32)],
        compiler_params=_cparams(("parallel", "parallel", "arbitrary")),
        name="moba_attention",
    )(h3, hb3, hb3, et, bias_near)
    return out.reshape(batch * seq, MOBA_W)


def _compress_kernel(a_ref, pet_ref, peb_ref, wt_ref, wb_ref, b1_ref, w2_ref, o_ref):
    a = a_ref[...]
    nrow = a.shape[0]
    top = _dot((a + pet_ref[...]).astype(_bf16), wt_ref[...])
    bot = _dot((a + peb_ref[...]).astype(_bf16), wb_ref[...])
    pre = top + pltpu.roll(bot, nrow - 1, 0) + b1_ref[...]
    hid = jax.nn.gelu(pre)
    o_ref[...] = _dot(hid.astype(_bf16), w2_ref[...])


def nsa_compress(h, batch, seq, pe, w1, b1, w2):
    nr = seq // NSA_CMP_STRIDE
    half = NSA_CMP_LEN // 2
    a = h[:, COL_KVC:COL_KVC + 128].reshape(batch, nr, NSA_CMP_STRIDE * 128)
    pe_cat = jnp.concatenate([pe[0], pe[1]], axis=-1)
    pe_top = pe_cat[:half].reshape(1, half * 128)
    pe_bot = pe_cat[half:].reshape(1, half * 128)
    hid = NSA_CMP_HIDDEN
    w_all = jnp.zeros((NSA_CMP_LEN, 128, 2 * hid), _f32)
    w_all = w_all.at[:, :HEAD_DIM, :hid].set(w1[0].reshape(NSA_CMP_LEN, HEAD_DIM, hid))
    w_all = w_all.at[:, HEAD_DIM:, hid:].set(w1[1].reshape(NSA_CMP_LEN, HEAD_DIM, hid))
    w_top = w_all[:half].reshape(half * 128, 2 * hid).astype(_bf16)
    w_bot = w_all[half:].reshape(half * 128, 2 * hid).astype(_bf16)
    b1cat = jnp.concatenate([b1[0], b1[1]]).reshape(1, 2 * hid)
    w2bd = jnp.zeros((2 * hid, 128), _f32)
    w2bd = w2bd.at[:hid, :HEAD_DIM].set(w2[0]).at[hid:, HEAD_DIM:].set(w2[1]).astype(_bf16)

    def const(shape):
        return pl.BlockSpec(shape, lambda b: (0,) * len(shape))

    return pl.pallas_call(
        _compress_kernel,
        out_shape=jax.ShapeDtypeStruct((batch, nr, 128), _f32),
        grid=(batch,),
        in_specs=[pl.BlockSpec((None, nr, half * 128), lambda b: (b, 0, 0)),
                  const(pe_top.shape), const(pe_bot.shape), const(w_top.shape), const(w_bot.shape),
                  const(b1cat.shape), const(w2bd.shape)],
        out_specs=pl.BlockSpec((None, nr, 128), lambda b: (b, 0, 0)),
        compiler_params=_cparams(("parallel",)),
        name="nsa_compress",
    )(a, pe_top, pe_bot, w_top, w_bot, b1cat, w2bd)


CMP_FIRST = PREV // NSA_CMP_STRIDE
CMP_NEAR = -(-((TQ + PREV - NSA_CMP_LEN) // NSA_CMP_STRIDE + 1) // 8) * 8
NSA_WKEYS = NSA_WINDOW + TQ
NSA_WTILES = NSA_WKEYS // PREV


def _nsa_tables(tab, seq):
    nc = seq // NSA_CMP_STRIDE
    nsel = seq // NSA_SEL_BLOCK
    n_cmp = (seq - NSA_CMP_LEN) // NSA_CMP_STRIDE + 1
    i = np.arange(TQ)[:, None]
    m = np.arange(CMP_NEAR)[None, :]
    d_c = i + (PREV - NSA_CMP_LEN + 1) - NSA_CMP_STRIDE * m
    cmp_bias = _bias_rel(tab, np.maximum(d_c, 0))
    k = np.arange(2 * NSA_WKEYS)
    g = _bias_rel(tab, np.where((k >= 1) & (k <= NSA_WINDOW), NSA_WINDOW - k, -1))
    win_bias = _toeplitz(g, TQ, NSA_WKEYS)
    n = np.arange(nc)[:, None]
    js = np.arange(nsel)[None, :]
    cs, ce = n * NSA_CMP_STRIDE, n * NSA_CMP_STRIDE + NSA_CMP_LEN
    ss = js * NSA_SEL_BLOCK
    ov = np.clip(np.minimum(ce, ss + NSA_SEL_BLOCK) - np.maximum(cs, ss), 0, None).astype(np.float32) / NSA_CMP_LEN
    ov = np.where(n < n_cmp, ov, 0.0)
    return cmp_bias, win_bias, _near_bias(tab, TQ), jnp.asarray(ov.T, _bf16)


def _exact_dot_r(x, r_bf16):
    hi = x.astype(_bf16)
    lo = (x - hi.astype(_f32)).astype(_bf16)
    return _dot(hi, r_bf16) + _dot(lo, r_bf16)


def _nsa_kernel(q_ref, gn_ref, kvc_ref, kvs_ref, kvw_ref, et_ref, cb_ref, wb_ref, nb_ref, ovt_ref, o_ref):
    c = pl.program_id(1)
    tq = TQ
    nh = NSA_HEADS
    nc = kvc_ref.shape[0]
    nsel = ovt_ref.shape[0]
    s0 = c * tq
    far_end = s0 - PREV
    lane = lax.broadcasted_iota(jnp.int32, (1, 128), 1)
    low = lane < HEAD_DIM

    q_all = q_ref[...] * (HEAD_DIM ** -0.5 * LOG2E)
    qh = []
    for h in range(nh):
        blk = q_all[:, 128 * (h // 2):128 * (h // 2) + 128]
        if h % 2 == 1:
            blk = pltpu.roll(blk, HEAD_DIM, 1)
        qh.append(jnp.where(low, blk, 0.0))
    q_b = jnp.concatenate(qh, axis=0).astype(_bf16)

    def ones_v(kv):
        return jnp.where(low, 1.0, kv).astype(_bf16)

    tiles = []
    for r in range(NSA_WTILES):
        start = pl.multiple_of(jnp.maximum(s0 - NSA_WINDOW + r * PREV, 0), PREV)
        tiles.append(kvw_ref[pl.ds(start, PREV), :])
    kvw_b = jnp.concatenate(tiles, axis=0)
    wk = NSA_WKEYS
    kpos_w = lax.broadcasted_iota(jnp.int32, (1, 1, wk), 2) + (s0 - NSA_WINDOW)
    s = _dot_nt(q_b, kvw_b).reshape(nh, tq, wk) + jnp.where(kpos_w >= 0, wb_ref[...], NEG)
    e = jnp.exp2(s - jnp.max(s, axis=2, keepdims=True))
    acc_w = _dot(e.reshape(nh * tq, wk).astype(_bf16), ones_v(kvw_b))
    o_w = (acc_w / acc_w[:, 0:1]).reshape(nh, tq, 128)

    kvc_b = kvc_ref[...].astype(_bf16)
    s_c = _dot_nt(q_b, kvc_b).reshape(nh, tq, nc)
    n_idx = lax.broadcasted_iota(jnp.int32, (tq, nc), 1)
    t_idx = lax.broadcasted_iota(jnp.int32, (tq, nc), 0) + s0
    mask_c = jnp.where((n_idx * NSA_CMP_STRIDE + (NSA_CMP_LEN - 1)) <= t_idx, 0.0, NEG)
    m_sel = lax.broadcasted_iota(jnp.int32, (CMP_NEAR, nc), 0)
    n_sel = lax.broadcasted_iota(jnp.int32, (CMP_NEAR, nc), 1)
    place = (n_sel == (s0 // NSA_CMP_STRIDE) - CMP_FIRST + m_sel).astype(_bf16)
    bias_c = _exact_dot_r(cb_ref[...].reshape(nh * tq, CMP_NEAR), place).reshape(nh, tq, nc)
    psum = jnp.zeros((tq, nc), _f32)
    o_c = []
    for h in range(nh):
        s = s_c[h] + bias_c[h] + mask_c
        mx = jnp.maximum(jnp.max(s, axis=1, keepdims=True), M_FLOOR)
        e = jnp.exp2(s - mx)
        p = e * (1.0 / jnp.maximum(jnp.sum(e, axis=1, keepdims=True), 1e-30))
        psum = psum + p
        o_c.append(_dot(p.astype(_bf16), kvc_b))

    imp_t = jnp.zeros((nsel, tq), _f32)
    for part in _split3(psum):
        imp_t = imp_t + _dot_nt(ovt_ref[...], part)
    jblk = lax.broadcasted_iota(jnp.int32, (nsel, tq), 0)
    blk_t = (lax.broadcasted_iota(jnp.int32, (nsel, tq), 1) + s0) // NSA_SEL_BLOCK
    forced = (jblk == 0) | (jblk == blk_t) | (jblk == blk_t - 1)
    work = jnp.where(forced, BIG, jnp.where(jblk <= blk_t, imp_t, -BIG))
    sel = jnp.zeros((nsel, tq), _f32)
    for _ in range(min(NSA_TOPN, nsel)):
        mx = jnp.max(work, axis=0, keepdims=True)
        idx = jnp.min(jnp.where(work == mx, jblk, nsel), axis=0, keepdims=True)
        pick = jblk == idx
        sel = jnp.where(pick, 1.0, sel)
        work = jnp.where(pick, -jnp.inf, work)

    eye = (lax.broadcasted_iota(jnp.int32, (nsel, 128), 0)
           == lax.broadcasted_iota(jnp.int32, (nsel, 128), 1)).astype(_bf16)
    notsel = _dot_tn((1.0 - sel).astype(_bf16), eye)
    notsel_far = jnp.where(lane < far_end // NSA_SEL_BLOCK, notsel, 1.0)
    qf = jnp.concatenate([q_b, jnp.concatenate([notsel_far.astype(_bf16)] * nh, axis=0)], axis=1)
    qn = jnp.concatenate([q_b, jnp.concatenate([notsel.astype(_bf16)] * nh, axis=0)], axis=1)

    def far_tile(kt, carry):
        rows = pl.ds(pl.multiple_of(kt * FAR_TILE, FAR_TILE), FAR_TILE)
        kv = kvs_ref[rows, :]
        s = _dot_nt(qf, jnp.concatenate([kv, et_ref[rows, :]], axis=1))
        return _softmax_update(*carry, s, ones_v(kv))

    n_far = (jnp.maximum(far_end, 0) + FAR_TILE - 1) // FAR_TILE
    init = (jnp.full((nh * tq, 1), M_FLOOR, _f32), jnp.zeros((nh * tq, 128), _f32))
    m, acc = _far_sweep(n_far, far_tile, init, exact_tail=True)

    prev = pl.ds(pl.multiple_of(jnp.maximum(far_end, 0), PREV), PREV)
    diag = pl.ds(pl.multiple_of(s0, tq), tq)
    kv = jnp.concatenate([kvs_ref[prev, :], kvs_ref[diag, :]], axis=0)
    et = jnp.concatenate([et_ref[prev, :], et_ref[diag, :]], axis=0)
    col = lax.broadcasted_iota(jnp.int32, (1, 1, PREV + tq), 2)
    nbias = jnp.where((col < PREV) & (c == 0), NEG, nb_ref[...])
    s = _dot_nt(qn, jnp.concatenate([kv, et], axis=1)) + nbias.reshape(nh * tq, PREV + tq)
    _, acc = _softmax_update(m, acc, s, ones_v(kv))
    o_s = (acc / acc[:, 0:1]).reshape(nh, tq, 128)

    g = jax.nn.sigmoid(gn_ref[...])
    outs = []
    for h in range(nh):
        outs.append(g[:, 3 * h:3 * h + 1] * o_c[h] + g[:, 3 * h + 1:3 * h + 2] * o_s[h]
                    + g[:, 3 * h + 2:3 * h + 3] * o_w[h])
    for hp in range(nh // 2):
        o_ref[:, 128 * hp:128 * hp + 128] = jnp.where(low, pltpu.roll(outs[2 * hp], HEAD_DIM, 1), outs[2 * hp + 1])


def nsa_attention(h, hb, kvc, batch, seq, tables):
    nq = seq // TQ
    nc = seq // NSA_CMP_STRIDE
    cmp_bias, win_bias, near_bias, ov_t = tables
    assert seq // NSA_SEL_BLOCK <= 128
    et = _block_onehot_neg(seq, NSA_SEL_BLOCK)

    def const(shape):
        return pl.BlockSpec(shape, lambda b, c: (0,) * len(shape))

    return pl.pallas_call(
        _nsa_kernel,
        out_shape=jax.ShapeDtypeStruct((batch * seq, NSA_W), _f32),
        grid=(batch, nq),
        in_specs=[pl.BlockSpec((TQ, NSA_W), lambda b, c: (b * nq + c, COL_QN // NSA_W)),
                  pl.BlockSpec((TQ, 128), lambda b, c: (b * nq + c, COL_GN // 128)),
                  pl.BlockSpec((None, nc, 128), lambda b, c: (b, 0, 0)),
                  pl.BlockSpec((seq, 128), lambda b, c: (b, COL_KVS // 128)),
                  pl.BlockSpec((seq, 128), lambda b, c: (b, COL_KVW // 128)),
                  const(et.shape),
                  const(cmp_bias.shape), const(win_bias.shape), const(near_bias.shape), const(ov_t.shape)],
        out_specs=pl.BlockSpec((TQ, NSA_W), lambda b, c: (b * nq + c, 0)),
        compiler_params=_cparams(("parallel", "arbitrary")),
        name="nsa_attention",
    )(h, h, kvc, hb, hb, et, cmp_bias, win_bias, near_bias, ov_t)


def kernel(x, w_in, nsa_cmp_pe, nsa_cmp_w1, nsa_cmp_b1, nsa_cmp_w2, conv_dw_w, conv_dw_b, conv_ln_g,
           conv_ln_b, conv_pw_w, hgrn_lb_logits, hgrn_norm_g, w_out, ln1_g, ln1_b, w_ff1, w_ff2,
           ln2_g, ln2_b, rel_bias):
    batch, seq, d = x.shape
    lb_sm = jax.nn.softmax(hgrn_lb_logits.astype(_f32), axis=0)
    lbs = jnp.cumsum(lb_sm, axis=0) - lb_sm[0]
    x2 = x.reshape(batch * seq, d)
    nsa_tables = _nsa_tables(rel_bias[:NSA_HEADS], seq)
    moba_near = _near_bias(rel_bias[NSA_HEADS:], MOBA_TQ)
    for l in range(DEPTH):
        h, hb = in_proj(x2, _regroup_w_in(w_in[l]))
        o_conv = conformer_conv(h, batch, seq, conv_dw_w[l], conv_dw_b[l], conv_ln_g[l], conv_ln_b[l],
                                conv_pw_w[l].astype(_bf16))
        o_hgrn = hgrn2(h, batch, seq, lbs[l], hgrn_norm_g[l])
        kvc = nsa_compress(h, batch, seq, nsa_cmp_pe[l], nsa_cmp_w1[l], nsa_cmp_b1[l], nsa_cmp_w2[l])
        o_nsa = nsa_attention(h, hb, kvc, batch, seq, nsa_tables)
        o_moba = moba_attention(h, hb, batch, seq, moba_near)
        x1 = out_proj_ln(o_nsa, o_moba, o_conv, o_hgrn, x2, w_out[l].astype(_bf16), ln1_g[l], ln1_b[l])
        x2 = mlp_ln(x1, w_ff1[l].astype(_bf16), w_ff2[l].astype(_bf16), ln2_g[l], ln2_b[l])
    return x2.reshape(batch, seq, d)
```

```python
import math

import jax
import jax.numpy as jnp
import numpy as np
from jax import lax
from jax.experimental import pallas as pl
from jax.experimental.pallas import tpu as pltpu

D_MODEL = 1024
DEPTH = 2
HEAD_DIM = 64
NSA_HEADS = 4
NSA_W = NSA_HEADS * HEAD_DIM
NSA_CMP_LEN = 32
NSA_CMP_STRIDE = 16
NSA_CMP_HIDDEN = 256
NSA_SEL_BLOCK = 64
NSA_TOPN = 16
NSA_WINDOW = 512
MOBA_HEADS = 4
MOBA_W = MOBA_HEADS * HEAD_DIM
MOBA_BLOCK = 256
MOBA_TOPK = 3
CONV_CH = 256
CONV_WIDTH = 31
HGRN_HEADS = 4
HGRN_KDIM = 64
HGRN_VDIM = 64
HGRN_W = HGRN_HEADS * HGRN_VDIM
HGRN_CHUNK = 64
MIX_WIDTH = NSA_W + MOBA_W + CONV_CH + HGRN_W
IN_WIDTHS = (NSA_W, HEAD_DIM, HEAD_DIM, HEAD_DIM, HEAD_DIM, HEAD_DIM, HEAD_DIM, 3 * NSA_HEADS,
             MOBA_W, MOBA_W, MOBA_W, 2 * CONV_CH, HGRN_W, HGRN_W, HGRN_W, HGRN_W)
D_FF = 4 * D_MODEL
REL_BUCKETS = 32
REL_MAX_EXACT = 16
REL_MAX_DIST = 128
TQ = 256
MOBA_TQ = 512
PAIR = 2
PREV = 128
LN_EPS = 1e-5
RMS_EPS = 1e-6
BIG = 1e9
DEEPNORM_ALPHA = (2 * DEPTH) ** 0.25

H_COLS = 3072
COL_QN = 0
COL_KVC = 256
COL_KVS = 384
COL_KVW = 512
COL_GN = 640
COL_QM = 768
COL_KM = 1024
COL_VM = 1280
COL_CONV = 1536
COL_QH = 2048
COL_FH = 2304
COL_IH = 2560
COL_GH = 2816

VMEM_LIMIT = 48 * 1024 * 1024
NEG = -1e30

_f32 = jnp.float32
_bf16 = jnp.bfloat16


def _cparams(sem):
    return pltpu.CompilerParams(dimension_semantics=sem, vmem_limit_bytes=VMEM_LIMIT)


def _layer_norm(z, g, b):
    mu = jnp.mean(z, axis=-1, keepdims=True)
    zc = z - mu
    var = jnp.mean(zc * zc, axis=-1, keepdims=True)
    return zc * lax.rsqrt(var + LN_EPS) * g + b


def _in_proj_kernel(x_ref, w_ref, o_ref, ob_ref):
    acc = jnp.dot(x_ref[...].astype(_bf16), w_ref[...], preferred_element_type=_f32)
    o_ref[...] = acc
    ob_ref[...] = acc.astype(_bf16)


def in_proj(x2, w_bf16, tm=512):
    n, d = x2.shape
    cols = w_bf16.shape[1]
    out = pl.BlockSpec((tm, cols), lambda i: (i, 0))
    return pl.pallas_call(
        _in_proj_kernel,
        out_shape=(jax.ShapeDtypeStruct((n, cols), _f32), jax.ShapeDtypeStruct((n, cols), _bf16)),
        grid=(n // tm,),
        in_specs=[pl.BlockSpec((tm, d), lambda i: (i, 0)),
                  pl.BlockSpec((d, cols), lambda i: (0, 0))],
        out_specs=(out, out),
        compiler_params=_cparams(("parallel",)),
        name="in_proj",
    )(x2, w_bf16)


def _regroup_w_in(w_in_l):
    gate_end = COL_GN + 3 * NSA_HEADS
    src = lax.broadcasted_iota(jnp.int32, (gate_end + H_COLS - COL_QM, H_COLS), 0)
    dst = lax.broadcasted_iota(jnp.int32, (gate_end + H_COLS - COL_QM, H_COLS), 1)
    place = (dst == jnp.where(src < gate_end, src, src + (COL_QM - gate_end))).astype(_bf16)
    assert place.shape[0] == w_in_l.shape[1] == sum(IN_WIDTHS)
    return jnp.dot(w_in_l.astype(_bf16), place, preferred_element_type=_f32).astype(_bf16)


def _out_proj_kernel(a_ref, b_ref, c_ref, d_ref, x_ref, w_ref, g_ref, beta_ref, o_ref):
    mixed = jnp.dot(a_ref[...].astype(_bf16), w_ref[0:256, :], preferred_element_type=_f32)
    mixed += jnp.dot(b_ref[...].astype(_bf16), w_ref[256:512, :], preferred_element_type=_f32)
    mixed += jnp.dot(c_ref[...].astype(_bf16), w_ref[512:768, :], preferred_element_type=_f32)
    mixed += jnp.dot(d_ref[...].astype(_bf16), w_ref[768:1024, :], preferred_element_type=_f32)
    z = DEEPNORM_ALPHA * x_ref[...] + mixed
    o_ref[...] = _layer_norm(z, g_ref[...], beta_ref[...])


def out_proj_ln(o_nsa, o_moba, o_conv, o_hgrn, x2, w_out_bf16, g, b, tm=512):
    n, d = x2.shape
    part = pl.BlockSpec((tm, 256), lambda i: (i, 0))
    vec = pl.BlockSpec((1, d), lambda i: (0, 0))
    return pl.pallas_call(
        _out_proj_kernel,
        out_shape=jax.ShapeDtypeStruct((n, d), _f32),
        grid=(n // tm,),
        in_specs=[part, part, part, part,
                  pl.BlockSpec((tm, d), lambda i: (i, 0)),
                  pl.BlockSpec((MIX_WIDTH, d), lambda i: (0, 0)),
                  vec, vec],
        out_specs=pl.BlockSpec((tm, d), lambda i: (i, 0)),
        compiler_params=_cparams(("parallel",)),
        name="out_proj_ln",
    )(o_nsa, o_moba, o_conv, o_hgrn, x2, w_out_bf16, g.reshape(1, d), b.reshape(1, d))


def _mlp_kernel(x_ref, w1_ref, w2_ref, g_ref, beta_ref, o_ref, acc_ref):
    j = pl.program_id(1)

    @pl.when(j == 0)
    def _():
        acc_ref[...] = jnp.zeros_like(acc_ref)

    hid = jnp.dot(x_ref[...].astype(_bf16), w1_ref[...], preferred_element_type=_f32)
    hid = jnp.square(jnp.maximum(hid, 0.0))
    acc_ref[...] += jnp.dot(hid.astype(_bf16), w2_ref[...], preferred_element_type=_f32)

    @pl.when(j == pl.num_programs(1) - 1)
    def _():
        z = DEEPNORM_ALPHA * x_ref[...] + acc_ref[...]
        o_ref[...] = _layer_norm(z, g_ref[...], beta_ref[...])


def mlp_ln(x1, w1_bf16, w2_bf16, g, b, tm=1024, tf=1024):
    n, d = x1.shape
    dff = w1_bf16.shape[1]
    vec = pl.BlockSpec((1, d), lambda i, j: (0, 0))
    return pl.pallas_call(
        _mlp_kernel,
        out_shape=jax.ShapeDtypeStruct((n, d), _f32),
        grid=(n // tm, dff // tf),
        in_specs=[pl.BlockSpec((tm, d), lambda i, j: (i, 0)),
                  pl.BlockSpec((d, tf), lambda i, j: (0, j)),
                  pl.BlockSpec((tf, d), lambda i, j: (j, 0)),
                  vec, vec],
        out_specs=pl.BlockSpec((tm, d), lambda i, j: (i, 0)),
        scratch_shapes=[pltpu.VMEM((tm, d), _f32)],
        compiler_params=_cparams(("parallel", "arbitrary")),
        name="mlp_ln",
    )(x1, w1_bf16, w2_bf16, g.reshape(1, d), b.reshape(1, d))


SUBLANES = 8
CONV_HALO = 32


def _conv_kernel(u_ref, wdw_ref, bdw_ref, g_ref, beta_ref, wpw_ref, o_ref, buf_ref, rot_ref):
    t = pl.program_id(1)
    tt = u_ref.shape[0]

    @pl.when(t == 0)
    def _():
        buf_ref[0:CONV_HALO, :] = jnp.zeros((CONV_HALO, CONV_CH), _f32)
        buf_ref[CONV_HALO + tt:, :] = jnp.zeros((SUBLANES, CONV_CH), _f32)

    @pl.when(t > 0)
    def _():
        buf_ref[0:CONV_HALO, :] = buf_ref[tt:tt + CONV_HALO, :]

    u = u_ref[...]
    a = u[:, :CONV_CH]
    gl = u[:, CONV_CH:]
    buf_ref[CONV_HALO:CONV_HALO + tt, :] = a * jax.nn.sigmoid(gl)

    acc = jnp.zeros((tt, CONV_CH), _f32) + bdw_ref[...]
    first = CONV_HALO - (CONV_WIDTH - 1)
    for r in range(SUBLANES):
        taps = [w for w in range(CONV_WIDTH) if (first + w) % SUBLANES == r]
        if not taps:
            continue
        rot_ref[r % 2] = buf_ref[r:r + CONV_HALO + tt, :]
        for w in taps:
            base = first + w - r
            acc = acc + rot_ref[r % 2, base:base + tt, :] * wdw_ref[w:w + 1, :]
    hn = _layer_norm(acc, g_ref[...], beta_ref[...])
    hn = hn * jax.nn.sigmoid(hn)
    o_ref[...] = jnp.dot(hn.astype(_bf16), wpw_ref[...], preferred_element_type=_f32)


def conformer_conv(h, batch, seq, w_dw, b_dw, ln_g, ln_b, w_pw_bf16, tt=512):
    nt = seq // tt
    vec = pl.BlockSpec((1, CONV_CH), lambda b, t: (0, 0))
    return pl.pallas_call(
        _conv_kernel,
        out_shape=jax.ShapeDtypeStruct((batch * seq, CONV_CH), _f32),
        grid=(batch, nt),
        in_specs=[pl.BlockSpec((tt, 2 * CONV_CH), lambda b, t: (b * nt + t, COL_CONV // (2 * CONV_CH))),
                  pl.BlockSpec((CONV_WIDTH, CONV_CH), lambda b, t: (0, 0)),
                  vec, vec, vec,
                  pl.BlockSpec((CONV_CH, CONV_CH), lambda b, t: (0, 0))],
        out_specs=pl.BlockSpec((tt, CONV_CH), lambda b, t: (b * nt + t, 0)),
        scratch_shapes=[pltpu.VMEM((CONV_HALO + tt + SUBLANES, CONV_CH), _f32),
                        pltpu.VMEM((2, CONV_HALO + tt, CONV_CH), _f32)],
        compiler_params=_cparams(("parallel", "arbitrary")),
        name="conformer_conv",
    )(h, w_dw, b_dw.reshape(1, CONV_CH), ln_g.reshape(1, CONV_CH), ln_b.reshape(1, CONV_CH), w_pw_bf16)


HGRN_LEVELS = (32, 16, 8, 4, 2, 1)
HGRN_TILE = 256


def _hgrn_constants():
    c = HGRN_CHUNK
    t = np.arange(c)[:, None]
    u = np.arange(c)[None, :]
    blocks = [(u <= t), (u > t)]
    masks = []
    for h in HGRN_LEVELS:
        mid = (t // (2 * h)) * (2 * h) + h
        upper = t >= mid
        blocks.append((upper & (u >= mid) & (u <= t)) | ((~upper) & (u > t) & (u <= mid - 1)))
        same = (t // (2 * h)) == (u // (2 * h))
        mid_s = (u // (2 * h)) * (2 * h) + h
        masks.append(same & upper & (u < mid_s))
    masks.append(t == u)
    lmat = np.concatenate(blocks, axis=0).astype(np.float32)
    mask = np.stack(masks).astype(np.float32)
    mask = np.tile(mask, (1, HGRN_HEADS, 1))
    lane_head = np.arange(HGRN_W)[None, :] // HGRN_KDIM
    lane_mask = (lane_head == np.arange(HGRN_HEADS)[:, None]).astype(np.float32)
    bd = (lane_head.T == lane_head).astype(np.float32)
    return lmat, mask, lane_mask, bd


def _split3(x):
    hi = x.astype(_bf16)
    r = x - hi.astype(_f32)
    mid = r.astype(_bf16)
    lo = (r - mid.astype(_f32)).astype(_bf16)
    return hi, mid, lo


def _dot(a, b):
    return jnp.dot(a, b, preferred_element_type=_f32)


def _dot_nt(a, b):
    return lax.dot_general(a, b, (((1,), (1,)), ((), ())), preferred_element_type=_f32)


def _dot_tn(a, b):
    return lax.dot_general(a, b, (((0,), (0,)), ((), ())), preferred_element_type=_f32)


def _exact_dot(l_bf16, x):
    hi = x.astype(_bf16)
    lo = (x - hi.astype(_f32)).astype(_bf16)
    return _dot(l_bf16, hi) + _dot(l_bf16, lo)


def _hgrn_kernel(q_ref, f_ref, i_ref, g_ref, lb_ref, ng_ref, lmat_ref, mask_ref, lm_ref, bd_ref,
                 o_ref, st_ref):
    c = HGRN_CHUNK
    nl = len(HGRN_LEVELS)

    @pl.when(pl.program_id(1) == 0)
    def _():
        st_ref[...] = jnp.zeros_like(st_ref)

    lb = lb_ref[...]
    bd = bd_ref[...]

    def chunk(ci):
        rows = pl.ds(ci * c, c)
        q = q_ref[rows, :]
        f = lb + (1.0 - lb) * jax.nn.sigmoid(f_ref[rows, :])
        logf = jnp.log(f)
        k = 1.0 - f
        v = i_ref[rows, :]
        v_b = v.astype(_bf16)
        ex = jnp.exp(_exact_dot(lmat_ref[...], logf))
        e_cum = ex[0:c]
        e_rest = ex[c:2 * c]

        attn = jnp.zeros((HGRN_HEADS * c, c), _f32)
        for l in range(nl + 1):
            if l < nl:
                el = ex[(2 + l) * c:(3 + l) * c]
                ql, kl = q * el, k * el
            else:
                ql, kl = q, k
            qs = jnp.concatenate([ql * lm_ref[h:h + 1, :] for h in range(HGRN_HEADS)], axis=0)
            attn = attn + _dot_nt(qs.astype(_bf16), kl.astype(_bf16)) * mask_ref[l]
        o = jnp.zeros((c, HGRN_W), _f32)
        for h in range(HGRN_HEADS):
            vh = (v * lm_ref[h:h + 1, :]).astype(_bf16)
            o = o + _dot(attn[h * c:(h + 1) * c, :].astype(_bf16), vh)

        st = st_ref[...]
        o = o + _dot_nt((q * e_cum).astype(_bf16), st.astype(_bf16))
        upd = _dot_tn(v_b, (k * e_rest).astype(_bf16)) * bd
        st_ref[...] = st * e_cum[c - 1:c, :] + upd

        osq = o * o
        hi = osq.astype(_bf16)
        lo = (osq - hi.astype(_f32)).astype(_bf16)
        bd_b = bd.astype(_bf16)
        ms = (_dot(hi, bd_b) + _dot(lo, bd_b)) * (1.0 / HGRN_VDIM)
        o = o * lax.rsqrt(ms + RMS_EPS)
        o_ref[rows, :] = o * ng_ref[...] * jax.nn.sigmoid(g_ref[rows, :])

    for ci in range(q_ref.shape[0] // c):
        chunk(ci)


def hgrn2(h, batch, seq, lb, norm_g):
    lmat, mask, lane_mask, bd = _hgrn_constants()
    tt = HGRN_TILE
    nt = seq // tt

    def col(cb):
        return pl.BlockSpec((tt, HGRN_W), lambda b, t: (b * nt + t, cb))

    def const(shape):
        nd = len(shape)
        return pl.BlockSpec(shape, lambda b, t: (0,) * nd)

    return pl.pallas_call(
        _hgrn_kernel,
        out_shape=jax.ShapeDtypeStruct((batch * seq, HGRN_W), _f32),
        grid=(batch, nt),
        in_specs=[col(COL_QH // HGRN_W), col(COL_FH // HGRN_W), col(COL_IH // HGRN_W), col(COL_GH // HGRN_W),
                  const((1, HGRN_W)), const((1, HGRN_W)),
                  const(lmat.shape), const(mask.shape), const(lane_mask.shape), const(bd.shape)],
        out_specs=pl.BlockSpec((tt, HGRN_W), lambda b, t: (b * nt + t, 0)),
        scratch_shapes=[pltpu.VMEM((HGRN_W, HGRN_W), _f32)],
        compiler_params=_cparams(("parallel", "arbitrary")),
        name="hgrn2",
    )(h, h, h, h, lb.reshape(1, HGRN_W), norm_g.reshape(1, HGRN_W),
      jnp.asarray(lmat, _bf16), jnp.asarray(mask), jnp.asarray(lane_mask), jnp.asarray(bd))


def _rel_bucket_np(dist):
    n = np.maximum(dist, 0)
    nf = np.maximum(n, 1).astype(np.float32)
    large = REL_MAX_EXACT + (np.log(nf / np.float32(REL_MAX_EXACT)) / np.float32(math.log(REL_MAX_DIST / REL_MAX_EXACT))
                             * np.float32(REL_BUCKETS - REL_MAX_EXACT)).astype(np.int32)
    return np.where(n < REL_MAX_EXACT, n, np.minimum(large, REL_BUCKETS - 1)).astype(np.int32)


FAR_DIST = 113
assert int(_rel_bucket_np(np.arange(FAR_DIST, 4 * FAR_DIST)).min()) == REL_BUCKETS - 1


LOG2E = 1.4426950408889634
M_FLOOR = -5e29


def _bias_rel(tab, dist):
    dist = np.asarray(dist)
    vals = jnp.take(tab, jnp.asarray(_rel_bucket_np(dist).reshape(-1)), axis=1)
    vals = (vals - tab[:, REL_BUCKETS - 1:]) * LOG2E
    vals = vals.reshape((tab.shape[0],) + dist.shape)
    return jnp.where(jnp.asarray(dist >= 0)[None], vals, NEG)


def _toeplitz_kernel(g_ref, o_ref):
    rows, width = o_ref.shape
    x = jnp.broadcast_to(g_ref[...], (rows, g_ref.shape[1]))
    o_ref[...] = pltpu.roll(x, 0, 1, stride=1, stride_axis=0)[:, :width]


def _toeplitz(g, rows, width):
    nh, length = g.shape
    return pl.pallas_call(
        _toeplitz_kernel,
        out_shape=jax.ShapeDtypeStruct((nh, rows, width), _f32),
        grid=(nh,),
        in_specs=[pl.BlockSpec((None, 1, length), lambda h: (h, 0, 0))],
        out_specs=pl.BlockSpec((None, rows, width), lambda h: (h, 0, 0)),
        compiler_params=_cparams(("parallel",)),
        name="toeplitz_bias",
    )(g.reshape(nh, 1, length))


def _near_bias(tab, tq):
    k = np.arange(2 * (PREV + tq))
    g = _bias_rel(tab, np.where(k < PREV + tq, PREV - k, 4 * FAR_DIST))
    return _toeplitz(g, tq, PREV + tq)


FAR_TILE = 512
FAR_UNROLL = 4
MOBA_UNROLL = 4


def _softmax_update(m, acc, s, rhs_b):
    m_new = jnp.maximum(m, jnp.max(s, axis=-1, keepdims=True))
    alpha = jnp.exp2(m - m_new)
    p = jnp.exp2(s - m_new)
    return m_new, alpha * acc + _dot(p.astype(_bf16), rhs_b)


def _far_sweep(n_tiles, tile_step, init, unroll, exact_tail):
    def body(it, carry):
        for u in range(unroll):
            carry = tile_step(it * unroll + u, carry)
        return carry

    if not exact_tail:
        return lax.fori_loop(0, (n_tiles + unroll - 1) // unroll, body, init)
    n_main = n_tiles // unroll
    carry = lax.fori_loop(0, n_main, body, init)
    return lax.fori_loop(n_main * unroll, n_tiles, tile_step, carry)


def _block_onehot_neg(seq, block):
    k = lax.broadcasted_iota(jnp.int32, (seq, 128), 0)
    lane = lax.broadcasted_iota(jnp.int32, (seq, 128), 1)
    return jnp.where(lane == k // block, NEG, 0.0).astype(_bf16)


MOBA_UNIT = PREV


def _moba_kernel(q_ref, k_ref, v_ref, et_ref, bn_ref, o_ref, kmean_ref):
    c = pl.program_id(2)
    seq = k_ref.shape[1]
    nb = seq // MOBA_BLOCK
    tq = MOBA_TQ

    @pl.when(c == 0)
    def _():
        for g in range(PAIR):
            kmean_ref[g] = jnp.sum(k_ref[g].astype(_f32).reshape(nb, MOBA_BLOCK, 128), axis=1) * (1.0 / MOBA_BLOCK)

    s0 = c * tq
    far_end = s0 - PREV
    n_far = (jnp.maximum(far_end, 0) + FAR_TILE - 1) // FAR_TILE

    lane = lax.broadcasted_iota(jnp.int32, (1, 128), 1)
    jcol = lax.broadcasted_iota(jnp.int32, (nb, tq), 0)
    own = (lax.broadcasted_iota(jnp.int32, (nb, tq), 1) + s0) // MOBA_BLOCK
    rep = (lax.broadcasted_iota(jnp.int32, (nb, 128), 1) // (MOBA_BLOCK // MOBA_UNIT)
           == lax.broadcasted_iota(jnp.int32, (nb, 128), 0)).astype(_bf16)

    qfs, qns = [], []
    for g in range(PAIR):
        q_all = q_ref[g] * (HEAD_DIM ** -0.5 * LOG2E)
        kmean_b = kmean_ref[g].astype(_bf16)
        q_far, q_near = [], []
        for hh in range(2):
            qh = jnp.where(lane // HEAD_DIM == hh, q_all, 0.0).astype(_bf16)
            gate = _dot_nt(kmean_b, qh)
            work = jnp.where(jcol < own, gate, -BIG)
            sel = jnp.zeros((nb, tq), _f32)
            for _ in range(MOBA_TOPK):
                mx = jnp.max(work, axis=0, keepdims=True)
                idx = jnp.min(jnp.where(work == mx, jcol, nb), axis=0, keepdims=True)
                pick = jcol == idx
                sel = jnp.where(pick & (mx > -0.5 * BIG), 1.0, sel)
                work = jnp.where(pick, -jnp.inf, work)
            sel = jnp.where(jcol == own, 1.0, sel)
            notsel = _dot_tn((1.0 - sel).astype(_bf16), rep)
            notsel_far = jnp.where(lane < far_end // MOBA_UNIT, notsel, 1.0)
            q_far.append(jnp.concatenate([qh, notsel_far.astype(_bf16)], axis=1))
            q_near.append(jnp.concatenate([qh, notsel.astype(_bf16)], axis=1))
        qfs.append(jnp.concatenate(q_far, axis=0))
        qns.append(jnp.concatenate(q_near, axis=0))
    ones = jnp.ones((max(FAR_TILE, PREV + tq), 128), _bf16)

    def far_tile(kt, carry):
        rows = pl.ds(pl.multiple_of(kt * FAR_TILE, FAR_TILE), FAR_TILE)
        et = et_ref[rows, :]
        out = []
        for g in range(PAIR):
            s = _dot_nt(qfs[g], jnp.concatenate([k_ref[g, rows, :], et], axis=1))
            out.append(_softmax_update(*carry[g], s, jnp.concatenate([v_ref[g, rows, :], ones[:FAR_TILE]], axis=1)))
        return tuple(out)

    init = tuple((jnp.full((2 * tq, 1), M_FLOOR, _f32), jnp.zeros((2 * tq, 256), _f32)) for _ in range(PAIR))
    state = _far_sweep(n_far, far_tile, init, MOBA_UNROLL, exact_tail=False)

    prev = pl.ds(pl.multiple_of(jnp.maximum(far_end, 0), PREV), PREV)
    diag = pl.ds(pl.multiple_of(s0, tq), tq)
    col = lax.broadcasted_iota(jnp.int32, (1, 1, PREV + tq), 2)
    nbias = jnp.where((col < PREV) & (c == 0), NEG, bn_ref[...])
    nbias = nbias.reshape(2 * tq, PREV + tq)
    et_n = jnp.concatenate([et_ref[prev, :], et_ref[diag, :]], axis=0)
    for g in range(PAIR):
        k_n = jnp.concatenate([jnp.concatenate([k_ref[g, prev, :], k_ref[g, diag, :]], axis=0), et_n], axis=1)
        v_n = jnp.concatenate([jnp.concatenate([v_ref[g, prev, :], v_ref[g, diag, :]], axis=0), ones[:PREV + tq]],
                              axis=1)
        _, acc = _softmax_update(*state[g], _dot_nt(qns[g], k_n) + nbias, v_n)
        out = acc[:, :128] / acc[:, 128:]
        o_ref[g] = jnp.where(lane // HEAD_DIM == 0, out[:tq], out[tq:])


def moba_attention(h, hb, batch, seq, bias_near):
    tq = MOBA_TQ
    nq = seq // tq
    assert seq // MOBA_UNIT <= 128 and tq % MOBA_BLOCK == 0 and batch % PAIR == 0
    et = _block_onehot_neg(seq, MOBA_UNIT)
    h3 = h.reshape(batch, seq, H_COLS)
    hb3 = hb.reshape(batch, seq, H_COLS)

    def kv(col):
        return pl.BlockSpec((PAIR, seq, 128), lambda b, hp, c: (b, 0, col // 128 + hp))

    out = pl.pallas_call(
        _moba_kernel,
        out_shape=jax.ShapeDtypeStruct((batch, seq, MOBA_W), _f32),
        grid=(batch // PAIR, MOBA_HEADS // 2, nq),
        in_specs=[pl.BlockSpec((PAIR, tq, 128), lambda b, hp, c: (b, c, COL_QM // 128 + hp)),
                  kv(COL_KM), kv(COL_VM),
                  pl.BlockSpec((seq, 128), lambda b, hp, c: (0, 0)),
                  pl.BlockSpec((2, tq, PREV + tq), lambda b, hp, c: (hp, 0, 0))],
        out_specs=pl.BlockSpec((PAIR, tq, 128), lambda b, hp, c: (b, c, hp)),
        scratch_shapes=[pltpu.VMEM((PAIR, seq // MOBA_BLOCK, 128), _f32)],
        compiler_params=_cparams(("parallel", "parallel", "arbitrary")),
        name="moba_attention",
    )(h3, hb3, hb3, et, bias_near)
    return out.reshape(batch * seq, MOBA_W)


def _compress_kernel(a_ref, pet_ref, peb_ref, wt_ref, wb_ref, b1_ref, w2_ref, o_ref):
    a = a_ref[...]
    nrow = a.shape[0]
    top = _dot((a + pet_ref[...]).astype(_bf16), wt_ref[...])
    bot = _dot((a + peb_ref[...]).astype(_bf16), wb_ref[...])
    pre = top + pltpu.roll(bot, nrow - 1, 0) + b1_ref[...]
    hid = jax.nn.gelu(pre)
    o_ref[...] = _dot(hid.astype(_bf16), w2_ref[...])


def nsa_compress(h, batch, seq, pe, w1, b1, w2):
    nr = seq // NSA_CMP_STRIDE
    half = NSA_CMP_LEN // 2
    a = h[:, COL_KVC:COL_KVC + 128].reshape(batch, nr, NSA_CMP_STRIDE * 128)
    pe_cat = jnp.concatenate([pe[0], pe[1]], axis=-1)
    pe_top = pe_cat[:half].reshape(1, half * 128)
    pe_bot = pe_cat[half:].reshape(1, half * 128)
    hid = NSA_CMP_HIDDEN
    w_all = jnp.zeros((NSA_CMP_LEN, 128, 2 * hid), _f32)
    w_all = w_all.at[:, :HEAD_DIM, :hid].set(w1[0].reshape(NSA_CMP_LEN, HEAD_DIM, hid))
    w_all = w_all.at[:, HEAD_DIM:, hid:].set(w1[1].reshape(NSA_CMP_LEN, HEAD_DIM, hid))
    w_top = w_all[:half].reshape(half * 128, 2 * hid).astype(_bf16)
    w_bot = w_all[half:].reshape(half * 128, 2 * hid).astype(_bf16)
    b1cat = jnp.concatenate([b1[0], b1[1]]).reshape(1, 2 * hid)
    w2bd = jnp.zeros((2 * hid, 128), _f32)
    w2bd = w2bd.at[:hid, :HEAD_DIM].set(w2[0]).at[hid:, HEAD_DIM:].set(w2[1]).astype(_bf16)

    def const(shape):
        return pl.BlockSpec(shape, lambda b: (0,) * len(shape))

    return pl.pallas_call(
        _compress_kernel,
        out_shape=jax.ShapeDtypeStruct((batch, nr, 128), _f32),
        grid=(batch,),
        in_specs=[pl.BlockSpec((None, nr, half * 128), lambda b: (b, 0, 0)),
                  const(pe_top.shape), const(pe_bot.shape), const(w_top.shape), const(w_bot.shape),
                  const(b1cat.shape), const(w2bd.shape)],
        out_specs=pl.BlockSpec((None, nr, 128), lambda b: (b, 0, 0)),
        compiler_params=_cparams(("parallel",)),
        name="nsa_compress",
    )(a, pe_top, pe_bot, w_top, w_bot, b1cat, w2bd)


CMP_FIRST = PREV // NSA_CMP_STRIDE
CMP_NEAR = -(-((TQ + PREV - NSA_CMP_LEN) // NSA_CMP_STRIDE + 1) // 8) * 8
NSA_WKEYS = NSA_WINDOW + TQ
NSA_WTILES = NSA_WKEYS // PREV


def _nsa_tables(tab, seq):
    nc = seq // NSA_CMP_STRIDE
    nsel = seq // NSA_SEL_BLOCK
    n_cmp = (seq - NSA_CMP_LEN) // NSA_CMP_STRIDE + 1
    i = np.arange(TQ)[:, None]
    m = np.arange(CMP_NEAR)[None, :]
    d_c = i + (PREV - NSA_CMP_LEN + 1) - NSA_CMP_STRIDE * m
    cmp_bias = _bias_rel(tab, np.maximum(d_c, 0))
    k = np.arange(2 * NSA_WKEYS)
    g = _bias_rel(tab, np.where((k >= 1) & (k <= NSA_WINDOW), NSA_WINDOW - k, -1))
    win_bias = _toeplitz(g, TQ, NSA_WKEYS)
    n = np.arange(nc)[:, None]
    js = np.arange(nsel)[None, :]
    cs, ce = n * NSA_CMP_STRIDE, n * NSA_CMP_STRIDE + NSA_CMP_LEN
    ss = js * NSA_SEL_BLOCK
    ov = np.clip(np.minimum(ce, ss + NSA_SEL_BLOCK) - np.maximum(cs, ss), 0, None).astype(np.float32) / NSA_CMP_LEN
    ov = np.where(n < n_cmp, ov, 0.0)
    return cmp_bias, win_bias, _near_bias(tab, TQ), jnp.asarray(ov.T, _bf16)


def _exact_dot_r(x, r_bf16):
    hi = x.astype(_bf16)
    lo = (x - hi.astype(_f32)).astype(_bf16)
    return _dot(hi, r_bf16) + _dot(lo, r_bf16)


def _nsa_kernel(q_ref, gn_ref, kvc_ref, kvs_ref, kvw_ref, et_ref, cb_ref, wb_ref, nb_ref, ovt_ref, o_ref):
    c = pl.program_id(1)
    tq = TQ
    nh = NSA_HEADS
    nc = kvc_ref.shape[0]
    nsel = ovt_ref.shape[0]
    s0 = c * tq
    far_end = s0 - PREV
    lane = lax.broadcasted_iota(jnp.int32, (1, 128), 1)
    low = lane < HEAD_DIM

    q_all = q_ref[...] * (HEAD_DIM ** -0.5 * LOG2E)
    qh = []
    for h in range(nh):
        blk = q_all[:, 128 * (h // 2):128 * (h // 2) + 128]
        if h % 2 == 1:
            blk = pltpu.roll(blk, HEAD_DIM, 1)
        qh.append(jnp.where(low, blk, 0.0))
    q_b = jnp.concatenate(qh, axis=0).astype(_bf16)

    def ones_v(kv):
        return jnp.where(low, 1.0, kv).astype(_bf16)

    kvc_b = kvc_ref[...].astype(_bf16)
    s_c = _dot_nt(q_b, kvc_b).reshape(nh, tq, nc)
    n_idx = lax.broadcasted_iota(jnp.int32, (tq, nc), 1)
    t_idx = lax.broadcasted_iota(jnp.int32, (tq, nc), 0) + s0
    mask_c = jnp.where((n_idx * NSA_CMP_STRIDE + (NSA_CMP_LEN - 1)) <= t_idx, 0.0, NEG)
    m_sel = lax.broadcasted_iota(jnp.int32, (CMP_NEAR, nc), 0)
    n_sel = lax.broadcasted_iota(jnp.int32, (CMP_NEAR, nc), 1)
    place = (n_sel == (s0 // NSA_CMP_STRIDE) - CMP_FIRST + m_sel).astype(_bf16)
    bias_c = _exact_dot_r(cb_ref[...].reshape(nh * tq, CMP_NEAR), place).reshape(nh, tq, nc)
    psum = jnp.zeros((tq, nc), _f32)
    o_c = []
    for h in range(nh):
        s = s_c[h] + bias_c[h] + mask_c
        mx = jnp.maximum(jnp.max(s, axis=1, keepdims=True), M_FLOOR)
        e = jnp.exp2(s - mx)
        p = e * (1.0 / jnp.maximum(jnp.sum(e, axis=1, keepdims=True), 1e-30))
        psum = psum + p
        o_c.append(_dot(p.astype(_bf16), kvc_b))

    imp_t = jnp.zeros((nsel, tq), _f32)
    for part in _split3(psum):
        imp_t = imp_t + _dot_nt(ovt_ref[...], part)
    jblk = lax.broadcasted_iota(jnp.int32, (nsel, tq), 0)
    blk_t = (lax.broadcasted_iota(jnp.int32, (nsel, tq), 1) + s0) // NSA_SEL_BLOCK
    forced = (jblk == 0) | (jblk == blk_t) | (jblk == blk_t - 1)
    work = jnp.where(forced, BIG, jnp.where(jblk <= blk_t, imp_t, -BIG))
    sel = jnp.zeros((nsel, tq), _f32)
    for _ in range(min(NSA_TOPN, nsel)):
        mx = jnp.max(work, axis=0, keepdims=True)
        idx = jnp.min(jnp.where(work == mx, jblk, nsel), axis=0, keepdims=True)
        pick = jblk == idx
        sel = jnp.where(pick, 1.0, sel)
        work = jnp.where(pick, -jnp.inf, work)

    eye = (lax.broadcasted_iota(jnp.int32, (nsel, 128), 0)
           == lax.broadcasted_iota(jnp.int32, (nsel, 128), 1)).astype(_bf16)
    notsel = _dot_tn((1.0 - sel).astype(_bf16), eye)
    notsel_far = jnp.where(lane < far_end // NSA_SEL_BLOCK, notsel, 1.0)
    qf = jnp.concatenate([q_b, jnp.concatenate([notsel_far.astype(_bf16)] * nh, axis=0)], axis=1)
    qn = jnp.concatenate([q_b, jnp.concatenate([notsel.astype(_bf16)] * nh, axis=0)], axis=1)

    def far_tile(kt, carry):
        rows = pl.ds(pl.multiple_of(kt * FAR_TILE, FAR_TILE), FAR_TILE)
        kv = kvs_ref[rows, :]
        s = _dot_nt(qf, jnp.concatenate([kv, et_ref[rows, :]], axis=1))
        return _softmax_update(*carry, s, ones_v(kv))

    n_far = (jnp.maximum(far_end, 0) + FAR_TILE - 1) // FAR_TILE
    init = (jnp.full((nh * tq, 1), M_FLOOR, _f32), jnp.zeros((nh * tq, 128), _f32))
    m, acc = _far_sweep(n_far, far_tile, init, FAR_UNROLL, exact_tail=True)

    prev = pl.ds(pl.multiple_of(jnp.maximum(far_end, 0), PREV), PREV)
    diag = pl.ds(pl.multiple_of(s0, tq), tq)
    kv = jnp.concatenate([kvs_ref[prev, :], kvs_ref[diag, :]], axis=0)
    et = jnp.concatenate([et_ref[prev, :], et_ref[diag, :]], axis=0)
    col = lax.broadcasted_iota(jnp.int32, (1, 1, PREV + tq), 2)
    nbias = jnp.where((col < PREV) & (c == 0), NEG, nb_ref[...])
    s = _dot_nt(qn, jnp.concatenate([kv, et], axis=1)) + nbias.reshape(nh * tq, PREV + tq)
    _, acc = _softmax_update(m, acc, s, ones_v(kv))
    o_s = (acc / acc[:, 0:1]).reshape(nh, tq, 128)

    tiles = []
    for r in range(NSA_WTILES):
        start = pl.multiple_of(jnp.maximum(s0 - NSA_WINDOW + r * PREV, 0), PREV)
        tiles.append(kvw_ref[pl.ds(start, PREV), :])
    kvw_b = jnp.concatenate(tiles, axis=0)
    wk = NSA_WKEYS
    kpos_w = lax.broadcasted_iota(jnp.int32, (1, 1, wk), 2) + (s0 - NSA_WINDOW)
    s = _dot_nt(q_b, kvw_b).reshape(nh, tq, wk) + jnp.where(kpos_w >= 0, wb_ref[...], NEG)
    e = jnp.exp2(s - jnp.max(s, axis=2, keepdims=True))
    acc_w = _dot(e.reshape(nh * tq, wk).astype(_bf16), ones_v(kvw_b))
    o_w = (acc_w / acc_w[:, 0:1]).reshape(nh, tq, 128)

    g = jax.nn.sigmoid(gn_ref[...])
    outs = []
    for h in range(nh):
        outs.append(g[:, 3 * h:3 * h + 1] * o_c[h] + g[:, 3 * h + 1:3 * h + 2] * o_s[h]
                    + g[:, 3 * h + 2:3 * h + 3] * o_w[h])
    for hp in range(nh // 2):
        o_ref[:, 128 * hp:128 * hp + 128] = jnp.where(low, pltpu.roll(outs[2 * hp], HEAD_DIM, 1), outs[2 * hp + 1])


def nsa_attention(h, hb, kvc, batch, seq, tables):
    nq = seq // TQ
    nc = seq // NSA_CMP_STRIDE
    cmp_bias, win_bias, near_bias, ov_t = tables
    assert seq // NSA_SEL_BLOCK <= 128
    et = _block_onehot_neg(seq, NSA_SEL_BLOCK)

    def const(shape):
        return pl.BlockSpec(shape, lambda b, c: (0,) * len(shape))

    return pl.pallas_call(
        _nsa_kernel,
        out_shape=jax.ShapeDtypeStruct((batch * seq, NSA_W), _f32),
        grid=(batch, nq),
        in_specs=[pl.BlockSpec((TQ, NSA_W), lambda b, c: (b * nq + c, COL_QN // NSA_W)),
                  pl.BlockSpec((TQ, 128), lambda b, c: (b * nq + c, COL_GN // 128)),
                  pl.BlockSpec((None, nc, 128), lambda b, c: (b, 0, 0)),
                  pl.BlockSpec((seq, 128), lambda b, c: (b, COL_KVS // 128)),
                  pl.BlockSpec((seq, 128), lambda b, c: (b, COL_KVW // 128)),
                  const(et.shape),
                  const(cmp_bias.shape), const(win_bias.shape), const(near_bias.shape), const(ov_t.shape)],
        out_specs=pl.BlockSpec((TQ, NSA_W), lambda b, c: (b * nq + c, 0)),
        compiler_params=_cparams(("parallel", "arbitrary")),
        name="nsa_attention",
    )(h, h, kvc, hb, hb, et, cmp_bias, win_bias, near_bias, ov_t)


def kernel(x, w_in, nsa_cmp_pe, nsa_cmp_w1, nsa_cmp_b1, nsa_cmp_w2, conv_dw_w, conv_dw_b, conv_ln_g,
           conv_ln_b, conv_pw_w, hgrn_lb_logits, hgrn_norm_g, w_out, ln1_g, ln1_b, w_ff1, w_ff2,
           ln2_g, ln2_b, rel_bias):
    batch, seq, d = x.shape
    lb_sm = jax.nn.softmax(hgrn_lb_logits.astype(_f32), axis=0)
    lbs = jnp.cumsum(lb_sm, axis=0) - lb_sm[0]
    x2 = x.reshape(batch * seq, d)
    nsa_tables = _nsa_tables(rel_bias[:NSA_HEADS], seq)
    moba_near = _near_bias(rel_bias[NSA_HEADS:], MOBA_TQ)
    for l in range(DEPTH):
        h, hb = in_proj(x2, _regroup_w_in(w_in[l]))
        o_conv = conformer_conv(h, batch, seq, conv_dw_w[l], conv_dw_b[l], conv_ln_g[l], conv_ln_b[l],
                                conv_pw_w[l].astype(_bf16))
        o_hgrn = hgrn2(h, batch, seq, lbs[l], hgrn_norm_g[l])
        kvc = nsa_compress(h, batch, seq, nsa_cmp_pe[l], nsa_cmp_w1[l], nsa_cmp_b1[l], nsa_cmp_w2[l])
        o_nsa = nsa_attention(h, hb, kvc, batch, seq, nsa_tables)
        o_moba = moba_attention(h, hb, batch, seq, moba_near)
        x1 = out_proj_ln(o_nsa, o_moba, o_conv, o_hgrn, x2, w_out[l].astype(_bf16), ln1_g[l], ln1_b[l])
        x2 = mlp_ln(x1, w_ff1[l].astype(_bf16), w_ff2[l].astype(_bf16), ln2_g[l], ln2_b[l])
    return x2.reshape(batch, seq, d)
```

```python
import math

import jax
import jax.numpy as jnp
import numpy as np
from jax import lax
from jax.experimental import pallas as pl
from jax.experimental.pallas import tpu as pltpu

D_MODEL = 1024
DEPTH = 2
HEAD_DIM = 64
NSA_HEADS = 4
NSA_W = NSA_HEADS * HEAD_DIM
NSA_CMP_LEN = 32
NSA_CMP_STRIDE = 16
NSA_CMP_HIDDEN = 256
NSA_SEL_BLOCK = 64
NSA_TOPN = 16
NSA_WINDOW = 512
MOBA_HEADS = 4
MOBA_W = MOBA_HEADS * HEAD_DIM
MOBA_BLOCK = 256
MOBA_TOPK = 3
CONV_CH = 256
CONV_WIDTH = 31
HGRN_HEADS = 4
HGRN_KDIM = 64
HGRN_VDIM = 64
HGRN_W = HGRN_HEADS * HGRN_VDIM
HGRN_CHUNK = 64
MIX_WIDTH = NSA_W + MOBA_W + CONV_CH + HGRN_W
IN_WIDTHS = (NSA_W, HEAD_DIM, HEAD_DIM, HEAD_DIM, HEAD_DIM, HEAD_DIM, HEAD_DIM, 3 * NSA_HEADS,
             MOBA_W, MOBA_W, MOBA_W, 2 * CONV_CH, HGRN_W, HGRN_W, HGRN_W, HGRN_W)
D_FF = 4 * D_MODEL
REL_BUCKETS = 32
REL_MAX_EXACT = 16
REL_MAX_DIST = 128
TQ = 256
MOBA_TQ = 512
PAIR = 2
PREV = 128
LN_EPS = 1e-5
RMS_EPS = 1e-6
BIG = 1e9
DEEPNORM_ALPHA = (2 * DEPTH) ** 0.25

H_COLS = 3072
COL_QN = 0
COL_KVC = 256
COL_KVS = 384
COL_KVW = 512
COL_GN = 640
COL_QM = 768
COL_KM = 1024
COL_VM = 1280
COL_CONV = 1536
COL_QH = 2048
COL_FH = 2304
COL_IH = 2560
COL_GH = 2816

VMEM_LIMIT = 48 * 1024 * 1024
NEG = -1e30

_f32 = jnp.float32
_bf16 = jnp.bfloat16


def _cparams(sem):
    return pltpu.CompilerParams(dimension_semantics=sem, vmem_limit_bytes=VMEM_LIMIT)


def _layer_norm(z, g, b):
    mu = jnp.mean(z, axis=-1, keepdims=True)
    zc = z - mu
    var = jnp.mean(zc * zc, axis=-1, keepdims=True)
    return zc * lax.rsqrt(var + LN_EPS) * g + b


def _in_proj_kernel(x_ref, w_ref, o_ref, ob_ref):
    acc = jnp.dot(x_ref[...].astype(_bf16), w_ref[...], preferred_element_type=_f32)
    o_ref[...] = acc
    ob_ref[...] = acc.astype(_bf16)


def in_proj(x2, w_bf16, tm=512):
    n, d = x2.shape
    cols = w_bf16.shape[1]
    out = pl.BlockSpec((tm, cols), lambda i: (i, 0))
    return pl.pallas_call(
        _in_proj_kernel,
        out_shape=(jax.ShapeDtypeStruct((n, cols), _f32), jax.ShapeDtypeStruct((n, cols), _bf16)),
        grid=(n // tm,),
        in_specs=[pl.BlockSpec((tm, d), lambda i: (i, 0)),
                  pl.BlockSpec((d, cols), lambda i: (0, 0))],
        out_specs=(out, out),
        compiler_params=_cparams(("parallel",)),
        name="in_proj",
    )(x2, w_bf16)


def _regroup_w_in(w_in_l):
    gate_end = COL_GN + 3 * NSA_HEADS
    src = lax.broadcasted_iota(jnp.int32, (gate_end + H_COLS - COL_QM, H_COLS), 0)
    dst = lax.broadcasted_iota(jnp.int32, (gate_end + H_COLS - COL_QM, H_COLS), 1)
    place = (dst == jnp.where(src < gate_end, src, src + (COL_QM - gate_end))).astype(_bf16)
    assert place.shape[0] == w_in_l.shape[1] == sum(IN_WIDTHS)
    return jnp.dot(w_in_l.astype(_bf16), place, preferred_element_type=_f32).astype(_bf16)


def _out_proj_kernel(a_ref, b_ref, c_ref, d_ref, x_ref, w_ref, g_ref, beta_ref, o_ref):
    mixed = jnp.dot(a_ref[...].astype(_bf16), w_ref[0:256, :], preferred_element_type=_f32)
    mixed += jnp.dot(b_ref[...].astype(_bf16), w_ref[256:512, :], preferred_element_type=_f32)
    mixed += jnp.dot(c_ref[...].astype(_bf16), w_ref[512:768, :], preferred_element_type=_f32)
    mixed += jnp.dot(d_ref[...].astype(_bf16), w_ref[768:1024, :], preferred_element_type=_f32)
    z = DEEPNORM_ALPHA * x_ref[...] + mixed
    o_ref[...] = _layer_norm(z, g_ref[...], beta_ref[...])


def out_proj_ln(o_nsa, o_moba, o_conv, o_hgrn, x2, w_out_bf16, g, b, tm=512):
    n, d = x2.shape
    part = pl.BlockSpec((tm, 256), lambda i: (i, 0))
    vec = pl.BlockSpec((1, d), lambda i: (0, 0))
    return pl.pallas_call(
        _out_proj_kernel,
        out_shape=jax.ShapeDtypeStruct((n, d), _f32),
        grid=(n // tm,),
        in_specs=[part, part, part, part,
                  pl.BlockSpec((tm, d), lambda i: (i, 0)),
                  pl.BlockSpec((MIX_WIDTH, d), lambda i: (0, 0)),
                  vec, vec],
        out_specs=pl.BlockSpec((tm, d), lambda i: (i, 0)),
        compiler_params=_cparams(("parallel",)),
        name="out_proj_ln",
    )(o_nsa, o_moba, o_conv, o_hgrn, x2, w_out_bf16, g.reshape(1, d), b.reshape(1, d))


def _mlp_kernel(x_ref, w1_ref, w2_ref, g_ref, beta_ref, o_ref, acc_ref):
    j = pl.program_id(1)

    @pl.when(j == 0)
    def _():
        acc_ref[...] = jnp.zeros_like(acc_ref)

    hid = jnp.dot(x_ref[...].astype(_bf16), w1_ref[...], preferred_element_type=_f32)
    hid = jnp.square(jnp.maximum(hid, 0.0))
    acc_ref[...] += jnp.dot(hid.astype(_bf16), w2_ref[...], preferred_element_type=_f32)

    @pl.when(j == pl.num_programs(1) - 1)
    def _():
        z = DEEPNORM_ALPHA * x_ref[...] + acc_ref[...]
        o_ref[...] = _layer_norm(z, g_ref[...], beta_ref[...])


def mlp_ln(x1, w1_bf16, w2_bf16, g, b, tm=1024, tf=1024):
    n, d = x1.shape
    dff = w1_bf16.shape[1]
    vec = pl.BlockSpec((1, d), lambda i, j: (0, 0))
    return pl.pallas_call(
        _mlp_kernel,
        out_shape=jax.ShapeDtypeStruct((n, d), _f32),
        grid=(n // tm, dff // tf),
        in_specs=[pl.BlockSpec((tm, d), lambda i, j: (i, 0)),
                  pl.BlockSpec((d, tf), lambda i, j: (0, j)),
                  pl.BlockSpec((tf, d), lambda i, j: (j, 0)),
                  vec, vec],
        out_specs=pl.BlockSpec((tm, d), lambda i, j: (i, 0)),
        scratch_shapes=[pltpu.VMEM((tm, d), _f32)],
        compiler_params=_cparams(("parallel", "arbitrary")),
        name="mlp_ln",
    )(x1, w1_bf16, w2_bf16, g.reshape(1, d), b.reshape(1, d))


SUBLANES = 8
CONV_HALO = 32


def _conv_kernel(u_ref, wdw_ref, bdw_ref, g_ref, beta_ref, wpw_ref, o_ref, buf_ref, rot_ref):
    t = pl.program_id(1)
    tt = u_ref.shape[0]

    @pl.when(t == 0)
    def _():
        buf_ref[0:CONV_HALO, :] = jnp.zeros((CONV_HALO, CONV_CH), _f32)
        buf_ref[CONV_HALO + tt:, :] = jnp.zeros((SUBLANES, CONV_CH), _f32)

    @pl.when(t > 0)
    def _():
        buf_ref[0:CONV_HALO, :] = buf_ref[tt:tt + CONV_HALO, :]

    u = u_ref[...]
    a = u[:, :CONV_CH]
    gl = u[:, CONV_CH:]
    buf_ref[CONV_HALO:CONV_HALO + tt, :] = a * jax.nn.sigmoid(gl)

    acc = jnp.zeros((tt, CONV_CH), _f32) + bdw_ref[...]
    first = CONV_HALO - (CONV_WIDTH - 1)
    for r in range(SUBLANES):
        taps = [w for w in range(CONV_WIDTH) if (first + w) % SUBLANES == r]
        if not taps:
            continue
        rot_ref[r % 2] = buf_ref[r:r + CONV_HALO + tt, :]
        for w in taps:
            base = first + w - r
            acc = acc + rot_ref[r % 2, base:base + tt, :] * wdw_ref[w:w + 1, :]
    hn = _layer_norm(acc, g_ref[...], beta_ref[...])
    hn = hn * jax.nn.sigmoid(hn)
    o_ref[...] = jnp.dot(hn.astype(_bf16), wpw_ref[...], preferred_element_type=_f32)


def conformer_conv(h, batch, seq, w_dw, b_dw, ln_g, ln_b, w_pw_bf16, tt=512):
    nt = seq // tt
    vec = pl.BlockSpec((1, CONV_CH), lambda b, t: (0, 0))
    return pl.pallas_call(
        _conv_kernel,
        out_shape=jax.ShapeDtypeStruct((batch * seq, CONV_CH), _f32),
        grid=(batch, nt),
        in_specs=[pl.BlockSpec((tt, 2 * CONV_CH), lambda b, t: (b * nt + t, COL_CONV // (2 * CONV_CH))),
                  pl.BlockSpec((CONV_WIDTH, CONV_CH), lambda b, t: (0, 0)),
                  vec, vec, vec,
                  pl.BlockSpec((CONV_CH, CONV_CH), lambda b, t: (0, 0))],
        out_specs=pl.BlockSpec((tt, CONV_CH), lambda b, t: (b * nt + t, 0)),
        scratch_shapes=[pltpu.VMEM((CONV_HALO + tt + SUBLANES, CONV_CH), _f32),
                        pltpu.VMEM((2, CONV_HALO + tt, CONV_CH), _f32)],
        compiler_params=_cparams(("parallel", "arbitrary")),
        name="conformer_conv",
    )(h, w_dw, b_dw.reshape(1, CONV_CH), ln_g.reshape(1, CONV_CH), ln_b.reshape(1, CONV_CH), w_pw_bf16)


HGRN_LEVELS = (32, 16, 8, 4, 2, 1)
HGRN_TILE = 256


def _hgrn_constants():
    c = HGRN_CHUNK
    t = np.arange(c)[:, None]
    u = np.arange(c)[None, :]
    masks = []
    for h in HGRN_LEVELS:
        mid = (t // (2 * h)) * (2 * h) + h
        upper = t >= mid
        same = (t // (2 * h)) == (u // (2 * h))
        mid_s = (u // (2 * h)) * (2 * h) + h
        masks.append(same & upper & (u < mid_s))
    masks.append(t == u)
    lmat = (u <= t).astype(np.float32)
    mask = np.stack(masks).astype(np.float32)
    mask = np.tile(mask, (1, HGRN_HEADS, 1))
    lane_head = np.arange(HGRN_W)[None, :] // HGRN_KDIM
    lane_mask = (lane_head == np.arange(HGRN_HEADS)[:, None]).astype(np.float32)
    bd = (lane_head.T == lane_head).astype(np.float32)
    return lmat, mask, lane_mask, bd


def _split3(x):
    hi = x.astype(_bf16)
    r = x - hi.astype(_f32)
    mid = r.astype(_bf16)
    lo = (r - mid.astype(_f32)).astype(_bf16)
    return hi, mid, lo


def _dot(a, b):
    return jnp.dot(a, b, preferred_element_type=_f32)


def _dot_nt(a, b):
    return lax.dot_general(a, b, (((1,), (1,)), ((), ())), preferred_element_type=_f32)


def _dot_tn(a, b):
    return lax.dot_general(a, b, (((0,), (0,)), ((), ())), preferred_element_type=_f32)


def _dot3(l_bf16, x):
    hi, mid, lo = _split3(x)
    return _dot(l_bf16, hi) + _dot(l_bf16, mid) + _dot(l_bf16, lo)


def _level_exponent(h, cum, logf, prev_g, next_g, row):
    c, w = cum.shape
    if h == 1:
        return jnp.where(row % 2 == 1, logf, 0.0)
    if h == 2:
        ph = row % 4
        return jnp.where(ph == 2, logf, jnp.where(ph == 3, logf + prev_g, jnp.where(ph == 0, next_g, 0.0)))
    pieces = [jnp.broadcast_to(cum[b * 2 * h + h - 1:b * 2 * h + h, :], (2 * h, w)) for b in range(c // (2 * h))]
    ref = pieces[0] if len(pieces) == 1 else jnp.concatenate(pieces, axis=0)
    return -jnp.abs(cum - ref)


def _hgrn_kernel(q_ref, f_ref, i_ref, g_ref, lb_ref, ng_ref, lmat_ref, mask_ref, lm_ref, bd_ref,
                 o_ref, st_ref):
    c = HGRN_CHUNK
    nl = len(HGRN_LEVELS)

    @pl.when(pl.program_id(1) == 0)
    def _():
        st_ref[...] = jnp.zeros_like(st_ref)

    lb = lb_ref[...]
    bd = bd_ref[...]
    row = lax.broadcasted_iota(jnp.int32, (c, 1), 0)

    def chunk(ci):
        rows = pl.ds(ci * c, c)
        q = q_ref[rows, :]
        f = lb + (1.0 - lb) * jax.nn.sigmoid(f_ref[rows, :])
        logf = jnp.log(f)
        k = 1.0 - f
        v = i_ref[rows, :]
        v_b = v.astype(_bf16)
        cum = _dot3(lmat_ref[...], logf)
        e_cum = jnp.exp(cum)
        e_rest = jnp.exp(cum[c - 1:c, :] - cum)
        prev_g = pltpu.roll(logf, 1, 0)
        next_g = pltpu.roll(logf, c - 1, 0)

        attn = jnp.zeros((HGRN_HEADS * c, c), _f32)
        for l in range(nl + 1):
            if l < nl:
                el = jnp.exp(_level_exponent(HGRN_LEVELS[l], cum, logf, prev_g, next_g, row))
                ql, kl = q * el, k * el
            else:
                ql, kl = q, k
            qs = jnp.concatenate([ql * lm_ref[h:h + 1, :] for h in range(HGRN_HEADS)], axis=0)
            attn = attn + _dot_nt(qs.astype(_bf16), kl.astype(_bf16)) * mask_ref[l]
        o = jnp.zeros((c, HGRN_W), _f32)
        for h in range(HGRN_HEADS):
            vh = (v * lm_ref[h:h + 1, :]).astype(_bf16)
            o = o + _dot(attn[h * c:(h + 1) * c, :].astype(_bf16), vh)

        st = st_ref[...]
        o = o + _dot_nt((q * e_cum).astype(_bf16), st.astype(_bf16))
        upd = _dot_tn(v_b, (k * e_rest).astype(_bf16)) * bd
        st_ref[...] = st * e_cum[c - 1:c, :] + upd

        osq = o * o
        hi = osq.astype(_bf16)
        lo = (osq - hi.astype(_f32)).astype(_bf16)
        bd_b = bd.astype(_bf16)
        ms = (_dot(hi, bd_b) + _dot(lo, bd_b)) * (1.0 / HGRN_VDIM)
        o = o * lax.rsqrt(ms + RMS_EPS)
        o_ref[rows, :] = o * ng_ref[...] * jax.nn.sigmoid(g_ref[rows, :])

    for ci in range(q_ref.shape[0] // c):
        chunk(ci)


def hgrn2(h, batch, seq, lb, norm_g):
    lmat, mask, lane_mask, bd = _hgrn_constants()
    tt = HGRN_TILE
    nt = seq // tt

    def col(cb):
        return pl.BlockSpec((tt, HGRN_W), lambda b, t: (b * nt + t, cb))

    def const(shape):
        nd = len(shape)
        return pl.BlockSpec(shape, lambda b, t: (0,) * nd)

    return pl.pallas_call(
        _hgrn_kernel,
        out_shape=jax.ShapeDtypeStruct((batch * seq, HGRN_W), _f32),
        grid=(batch, nt),
        in_specs=[col(COL_QH // HGRN_W), col(COL_FH // HGRN_W), col(COL_IH // HGRN_W), col(COL_GH // HGRN_W),
                  const((1, HGRN_W)), const((1, HGRN_W)),
                  const(lmat.shape), const(mask.shape), const(lane_mask.shape), const(bd.shape)],
        out_specs=pl.BlockSpec((tt, HGRN_W), lambda b, t: (b * nt + t, 0)),
        scratch_shapes=[pltpu.VMEM((HGRN_W, HGRN_W), _f32)],
        compiler_params=_cparams(("parallel", "arbitrary")),
        name="hgrn2",
    )(h, h, h, h, lb.reshape(1, HGRN_W), norm_g.reshape(1, HGRN_W),
      jnp.asarray(lmat, _bf16), jnp.asarray(mask), jnp.asarray(lane_mask), jnp.asarray(bd))


def _rel_bucket_np(dist):
    n = np.maximum(dist, 0)
    nf = np.maximum(n, 1).astype(np.float32)
    large = REL_MAX_EXACT + (np.log(nf / np.float32(REL_MAX_EXACT)) / np.float32(math.log(REL_MAX_DIST / REL_MAX_EXACT))
                             * np.float32(REL_BUCKETS - REL_MAX_EXACT)).astype(np.int32)
    return np.where(n < REL_MAX_EXACT, n, np.minimum(large, REL_BUCKETS - 1)).astype(np.int32)


FAR_DIST = 113
assert int(_rel_bucket_np(np.arange(FAR_DIST, 4 * FAR_DIST)).min()) == REL_BUCKETS - 1


LOG2E = 1.4426950408889634
M_FLOOR = -5e29


def _bias_rel(tab, dist):
    dist = np.asarray(dist)
    vals = jnp.take(tab, jnp.asarray(_rel_bucket_np(dist).reshape(-1)), axis=1)
    vals = (vals - tab[:, REL_BUCKETS - 1:]) * LOG2E
    vals = vals.reshape((tab.shape[0],) + dist.shape)
    return jnp.where(jnp.asarray(dist >= 0)[None], vals, NEG)


def _toeplitz_kernel(g_ref, o_ref):
    rows, width = o_ref.shape
    x = jnp.broadcast_to(g_ref[...], (rows, g_ref.shape[1]))
    o_ref[...] = pltpu.roll(x, 0, 1, stride=1, stride_axis=0)[:, :width]


def _toeplitz(g, rows, width):
    nh, length = g.shape
    return pl.pallas_call(
        _toeplitz_kernel,
        out_shape=jax.ShapeDtypeStruct((nh, rows, width), _f32),
        grid=(nh,),
        in_specs=[pl.BlockSpec((None, 1, length), lambda h: (h, 0, 0))],
        out_specs=pl.BlockSpec((None, rows, width), lambda h: (h, 0, 0)),
        compiler_params=_cparams(("parallel",)),
        name="toeplitz_bias",
    )(g.reshape(nh, 1, length))


def _near_bias(tab, tq):
    k = np.arange(2 * (PREV + tq))
    g = _bias_rel(tab, np.where(k < PREV + tq, PREV - k, 4 * FAR_DIST))
    return _toeplitz(g, tq, PREV + tq)


FAR_TILE = 512
FAR_UNROLL = 4
MOBA_UNROLL = 4


def _softmax_update(m, acc, s, rhs_b):
    m_new = jnp.maximum(m, jnp.max(s, axis=-1, keepdims=True))
    alpha = jnp.exp2(m - m_new)
    p = jnp.exp2(s - m_new)
    return m_new, alpha * acc + _dot(p.astype(_bf16), rhs_b)


def _far_sweep(n_tiles, tile_step, init, unroll, exact_tail):
    def body(it, carry):
        for u in range(unroll):
            carry = tile_step(it * unroll + u, carry)
        return carry

    if not exact_tail:
        return lax.fori_loop(0, (n_tiles + unroll - 1) // unroll, body, init)
    n_main = n_tiles // unroll
    carry = lax.fori_loop(0, n_main, body, init)
    return lax.fori_loop(n_main * unroll, n_tiles, tile_step, carry)


def _block_onehot_neg(seq, block):
    k = lax.broadcasted_iota(jnp.int32, (seq, 128), 0)
    lane = lax.broadcasted_iota(jnp.int32, (seq, 128), 1)
    return jnp.where(lane == k // block, NEG, 0.0).astype(_bf16)


MOBA_UNIT = PREV


def _moba_kernel(q_ref, k_ref, v_ref, et_ref, bn_ref, o_ref, kmean_ref):
    c = pl.program_id(2)
    seq = k_ref.shape[1]
    nb = seq // MOBA_BLOCK
    tq = MOBA_TQ

    @pl.when(c == 0)
    def _():
        for g in range(PAIR):
            kmean_ref[g] = jnp.sum(k_ref[g].astype(_f32).reshape(nb, MOBA_BLOCK, 128), axis=1) * (1.0 / MOBA_BLOCK)

    s0 = c * tq
    far_end = s0 - PREV
    n_far = (jnp.maximum(far_end, 0) + FAR_TILE - 1) // FAR_TILE

    lane = lax.broadcasted_iota(jnp.int32, (1, 128), 1)
    jcol = lax.broadcasted_iota(jnp.int32, (nb, tq), 0)
    own = (lax.broadcasted_iota(jnp.int32, (nb, tq), 1) + s0) // MOBA_BLOCK
    rep = (lax.broadcasted_iota(jnp.int32, (nb, 128), 1) // (MOBA_BLOCK // MOBA_UNIT)
           == lax.broadcasted_iota(jnp.int32, (nb, 128), 0)).astype(_bf16)

    qfs, qns = [], []
    for g in range(PAIR):
        q_all = q_ref[g] * (HEAD_DIM ** -0.5 * LOG2E)
        kmean_b = kmean_ref[g].astype(_bf16)
        q_far, q_near = [], []
        for hh in range(2):
            qh = jnp.where(lane // HEAD_DIM == hh, q_all, 0.0).astype(_bf16)
            gate = _dot_nt(kmean_b, qh)
            work = jnp.where(jcol < own, gate, -BIG)
            sel = jnp.zeros((nb, tq), _f32)
            for _ in range(MOBA_TOPK):
                mx = jnp.max(work, axis=0, keepdims=True)
                idx = jnp.min(jnp.where(work == mx, jcol, nb), axis=0, keepdims=True)
                pick = jcol == idx
                sel = jnp.where(pick & (mx > -0.5 * BIG * LOG2E), 1.0, sel)
                work = jnp.where(pick, -jnp.inf, work)
            sel = jnp.where(jcol == own, 1.0, sel)
            notsel = _dot_tn((1.0 - sel).astype(_bf16), rep)
            notsel_far = jnp.where(lane < far_end // MOBA_UNIT, notsel, 1.0)
            q_far.append(jnp.concatenate([qh, notsel_far.astype(_bf16)], axis=1))
            q_near.append(jnp.concatenate([qh, notsel.astype(_bf16)], axis=1))
        qfs.append(jnp.concatenate(q_far, axis=0))
        qns.append(jnp.concatenate(q_near, axis=0))
    ones = jnp.ones((max(FAR_TILE, PREV + tq), 128), _bf16)

    def far_tile(kt, carry):
        rows = pl.ds(pl.multiple_of(kt * FAR_TILE, FAR_TILE), FAR_TILE)
        et = et_ref[rows, :]
        out = []
        for g in range(PAIR):
            s = _dot_nt(qfs[g], jnp.concatenate([k_ref[g, rows, :], et], axis=1))
            out.append(_softmax_update(*carry[g], s, jnp.concatenate([v_ref[g, rows, :], ones[:FAR_TILE]], axis=1)))
        return tuple(out)

    init = tuple((jnp.full((2 * tq, 1), M_FLOOR, _f32), jnp.zeros((2 * tq, 256), _f32)) for _ in range(PAIR))
    state = _far_sweep(n_far, far_tile, init, MOBA_UNROLL, exact_tail=False)

    prev = pl.ds(pl.multiple_of(jnp.maximum(far_end, 0), PREV), PREV)
    diag = pl.ds(pl.multiple_of(s0, tq), tq)
    col = lax.broadcasted_iota(jnp.int32, (1, 1, PREV + tq), 2)
    nbias = jnp.where((col < PREV) & (c == 0), NEG, bn_ref[...])
    nbias = nbias.reshape(2 * tq, PREV + tq)
    et_n = jnp.concatenate([et_ref[prev, :], et_ref[diag, :]], axis=0)
    for g in range(PAIR):
        k_n = jnp.concatenate([jnp.concatenate([k_ref[g, prev, :], k_ref[g, diag, :]], axis=0), et_n], axis=1)
        v_n = jnp.concatenate([jnp.concatenate([v_ref[g, prev, :], v_ref[g, diag, :]], axis=0), ones[:PREV + tq]],
                              axis=1)
        _, acc = _softmax_update(*state[g], _dot_nt(qns[g], k_n) + nbias, v_n)
        out = acc[:, :128] / acc[:, 128:]
        o_ref[g] = jnp.where(lane // HEAD_DIM == 0, out[:tq], out[tq:])


def moba_attention(h, hb, batch, seq, bias_near):
    tq = MOBA_TQ
    nq = seq // tq
    assert seq // MOBA_UNIT <= 128 and tq % MOBA_BLOCK == 0 and batch % PAIR == 0
    et = _block_onehot_neg(seq, MOBA_UNIT)
    h3 = h.reshape(batch, seq, H_COLS)
    hb3 = hb.reshape(batch, seq, H_COLS)

    def kv(col):
        return pl.BlockSpec((PAIR, seq, 128), lambda b, hp, c: (b, 0, col // 128 + hp))

    out = pl.pallas_call(
        _moba_kernel,
        out_shape=jax.ShapeDtypeStruct((batch, seq, MOBA_W), _f32),
        grid=(batch // PAIR, MOBA_HEADS // 2, nq),
        in_specs=[pl.BlockSpec((PAIR, tq, 128), lambda b, hp, c: (b, c, COL_QM // 128 + hp)),
                  kv(COL_KM), kv(COL_VM),
                  pl.BlockSpec((seq, 128), lambda b, hp, c: (0, 0)),
                  pl.BlockSpec((2, tq, PREV + tq), lambda b, hp, c: (hp, 0, 0))],
        out_specs=pl.BlockSpec((PAIR, tq, 128), lambda b, hp, c: (b, c, hp)),
        scratch_shapes=[pltpu.VMEM((PAIR, seq // MOBA_BLOCK, 128), _f32)],
        compiler_params=_cparams(("parallel", "parallel", "arbitrary")),
        name="moba_attention",
    )(h3, hb3, hb3, et, bias_near)
    return out.reshape(batch * seq, MOBA_W)


def _compress_kernel(a_ref, pet_ref, peb_ref, wt_ref, wb_ref, b1_ref, w2_ref, o_ref):
    a = a_ref[...]
    nrow = a.shape[0]
    top = _dot((a + pet_ref[...]).astype(_bf16), wt_ref[...])
    bot = _dot((a + peb_ref[...]).astype(_bf16), wb_ref[...])
    pre = top + pltpu.roll(bot, nrow - 1, 0) + b1_ref[...]
    hid = jax.nn.gelu(pre)
    o_ref[...] = _dot(hid.astype(_bf16), w2_ref[...])


def nsa_compress(h, batch, seq, pe, w1, b1, w2):
    nr = seq // NSA_CMP_STRIDE
    half = NSA_CMP_LEN // 2
    a = h[:, COL_KVC:COL_KVC + 128].reshape(batch, nr, NSA_CMP_STRIDE * 128)
    pe_cat = jnp.concatenate([pe[0], pe[1]], axis=-1)
    pe_top = pe_cat[:half].reshape(1, half * 128)
    pe_bot = pe_cat[half:].reshape(1, half * 128)
    hid = NSA_CMP_HIDDEN
    w_all = jnp.zeros((NSA_CMP_LEN, 128, 2 * hid), _f32)
    w_all = w_all.at[:, :HEAD_DIM, :hid].set(w1[0].reshape(NSA_CMP_LEN, HEAD_DIM, hid))
    w_all = w_all.at[:, HEAD_DIM:, hid:].set(w1[1].reshape(NSA_CMP_LEN, HEAD_DIM, hid))
    w_top = w_all[:half].reshape(half * 128, 2 * hid).astype(_bf16)
    w_bot = w_all[half:].reshape(half * 128, 2 * hid).astype(_bf16)
    b1cat = jnp.concatenate([b1[0], b1[1]]).reshape(1, 2 * hid)
    w2bd = jnp.zeros((2 * hid, 128), _f32)
    w2bd = w2bd.at[:hid, :HEAD_DIM].set(w2[0]).at[hid:, HEAD_DIM:].set(w2[1]).astype(_bf16)

    def const(shape):
        return pl.BlockSpec(shape, lambda b: (0,) * len(shape))

    return pl.pallas_call(
        _compress_kernel,
        out_shape=jax.ShapeDtypeStruct((batch, nr, 128), _f32),
        grid=(batch,),
        in_specs=[pl.BlockSpec((None, nr, half * 128), lambda b: (b, 0, 0)),
                  const(pe_top.shape), const(pe_bot.shape), const(w_top.shape), const(w_bot.shape),
                  const(b1cat.shape), const(w2bd.shape)],
        out_specs=pl.BlockSpec((None, nr, 128), lambda b: (b, 0, 0)),
        compiler_params=_cparams(("parallel",)),
        name="nsa_compress",
    )(a, pe_top, pe_bot, w_top, w_bot, b1cat, w2bd)


CMP_FIRST = PREV // NSA_CMP_STRIDE
CMP_NEAR = -(-((TQ + PREV - NSA_CMP_LEN) // NSA_CMP_STRIDE + 1) // 8) * 8
NSA_WKEYS = NSA_WINDOW + TQ
NSA_WTILES = NSA_WKEYS // PREV


def _nsa_tables(tab, seq):
    nc = seq // NSA_CMP_STRIDE
    nsel = seq // NSA_SEL_BLOCK
    n_cmp = (seq - NSA_CMP_LEN) // NSA_CMP_STRIDE + 1
    i = np.arange(TQ)[:, None]
    m = np.arange(CMP_NEAR)[None, :]
    d_c = i + (PREV - NSA_CMP_LEN + 1) - NSA_CMP_STRIDE * m
    cmp_bias = _bias_rel(tab, np.maximum(d_c, 0))
    k = np.arange(2 * NSA_WKEYS)
    g = _bias_rel(tab, np.where((k >= 1) & (k <= NSA_WINDOW), NSA_WINDOW - k, -1))
    win_bias = _toeplitz(g, TQ, NSA_WKEYS)
    n = np.arange(nc)[:, None]
    js = np.arange(nsel)[None, :]
    cs, ce = n * NSA_CMP_STRIDE, n * NSA_CMP_STRIDE + NSA_CMP_LEN
    ss = js * NSA_SEL_BLOCK
    ov = np.clip(np.minimum(ce, ss + NSA_SEL_BLOCK) - np.maximum(cs, ss), 0, None).astype(np.float32) / NSA_CMP_LEN
    ov = np.where(n < n_cmp, ov, 0.0)
    return cmp_bias, win_bias, _near_bias(tab, TQ), jnp.asarray(ov.T, _bf16)


def _exact_dot_r(x, r_bf16):
    hi = x.astype(_bf16)
    lo = (x - hi.astype(_f32)).astype(_bf16)
    return _dot(hi, r_bf16) + _dot(lo, r_bf16)


def _nsa_kernel(q_ref, gn_ref, kvc_ref, kvs_ref, kvw_ref, et_ref, cb_ref, wb_ref, nb_ref, ovt_ref, o_ref):
    c = pl.program_id(1)
    tq = TQ
    nh = NSA_HEADS
    nc = kvc_ref.shape[0]
    nsel = ovt_ref.shape[0]
    s0 = c * tq
    far_end = s0 - PREV
    lane = lax.broadcasted_iota(jnp.int32, (1, 128), 1)
    low = lane < HEAD_DIM

    q_all = q_ref[...] * (HEAD_DIM ** -0.5 * LOG2E)
    qh = []
    for h in range(nh):
        blk = q_all[:, 128 * (h // 2):128 * (h // 2) + 128]
        if h % 2 == 1:
            blk = pltpu.roll(blk, HEAD_DIM, 1)
        qh.append(jnp.where(low, blk, 0.0))
    q_b = jnp.concatenate(qh, axis=0).astype(_bf16)

    def ones_v(kv):
        return jnp.where(low, 1.0, kv).astype(_bf16)

    kvc_b = kvc_ref[...].astype(_bf16)
    s_c = _dot_nt(q_b, kvc_b).reshape(nh, tq, nc)
    n_idx = lax.broadcasted_iota(jnp.int32, (tq, nc), 1)
    t_idx = lax.broadcasted_iota(jnp.int32, (tq, nc), 0) + s0
    mask_c = jnp.where((n_idx * NSA_CMP_STRIDE + (NSA_CMP_LEN - 1)) <= t_idx, 0.0, NEG)
    m_sel = lax.broadcasted_iota(jnp.int32, (CMP_NEAR, nc), 0)
    n_sel = lax.broadcasted_iota(jnp.int32, (CMP_NEAR, nc), 1)
    place = (n_sel == (s0 // NSA_CMP_STRIDE) - CMP_FIRST + m_sel).astype(_bf16)
    bias_c = _exact_dot_r(cb_ref[...].reshape(nh * tq, CMP_NEAR), place).reshape(nh, tq, nc)
    psum = jnp.zeros((tq, nc), _f32)
    o_c = []
    for h in range(nh):
        s = s_c[h] + bias_c[h] + mask_c
        mx = jnp.maximum(jnp.max(s, axis=1, keepdims=True), M_FLOOR)
        e = jnp.exp2(s - mx)
        p = e * (1.0 / jnp.maximum(jnp.sum(e, axis=1, keepdims=True), 1e-30))
        psum = psum + p
        o_c.append(_dot(p.astype(_bf16), kvc_b))

    imp_t = jnp.zeros((nsel, tq), _f32)
    for part in _split3(psum):
        imp_t = imp_t + _dot_nt(ovt_ref[...], part)
    jblk = lax.broadcasted_iota(jnp.int32, (nsel, tq), 0)
    blk_t = (lax.broadcasted_iota(jnp.int32, (nsel, tq), 1) + s0) // NSA_SEL_BLOCK
    forced = (jblk == 0) | (jblk == blk_t) | (jblk == blk_t - 1)
    work = jnp.where(forced, BIG, jnp.where(jblk <= blk_t, imp_t, -BIG))
    sel = jnp.zeros((nsel, tq), _f32)
    for _ in range(min(NSA_TOPN, nsel)):
        mx = jnp.max(work, axis=0, keepdims=True)
        idx = jnp.min(jnp.where(work == mx, jblk, nsel), axis=0, keepdims=True)
        pick = jblk == idx
        sel = jnp.where(pick, 1.0, sel)
        work = jnp.where(pick, -jnp.inf, work)

    eye = (lax.broadcasted_iota(jnp.int32, (nsel, 128), 0)
           == lax.broadcasted_iota(jnp.int32, (nsel, 128), 1)).astype(_bf16)
    notsel = _dot_tn((1.0 - sel).astype(_bf16), eye)
    notsel_far = jnp.where(lane < far_end // NSA_SEL_BLOCK, notsel, 1.0)
    qf = jnp.concatenate([q_b, jnp.concatenate([notsel_far.astype(_bf16)] * nh, axis=0)], axis=1)
    qn = jnp.concatenate([q_b, jnp.concatenate([notsel.astype(_bf16)] * nh, axis=0)], axis=1)

    def far_tile(kt, carry):
        rows = pl.ds(pl.multiple_of(kt * FAR_TILE, FAR_TILE), FAR_TILE)
        kv = kvs_ref[rows, :]
        s = _dot_nt(qf, jnp.concatenate([kv, et_ref[rows, :]], axis=1))
        return _softmax_update(*carry, s, ones_v(kv))

    n_far = (jnp.maximum(far_end, 0) + FAR_TILE - 1) // FAR_TILE
    init = (jnp.full((nh * tq, 1), M_FLOOR, _f32), jnp.zeros((nh * tq, 128), _f32))
    m, acc = _far_sweep(n_far, far_tile, init, FAR_UNROLL, exact_tail=True)

    prev = pl.ds(pl.multiple_of(jnp.maximum(far_end, 0), PREV), PREV)
    diag = pl.ds(pl.multiple_of(s0, tq), tq)
    kv = jnp.concatenate([kvs_ref[prev, :], kvs_ref[diag, :]], axis=0)
    et = jnp.concatenate([et_ref[prev, :], et_ref[diag, :]], axis=0)
    col = lax.broadcasted_iota(jnp.int32, (1, 1, PREV + tq), 2)
    nbias = jnp.where((col < PREV) & (c == 0), NEG, nb_ref[...])
    s = _dot_nt(qn, jnp.concatenate([kv, et], axis=1)) + nbias.reshape(nh * tq, PREV + tq)
    _, acc = _softmax_update(m, acc, s, ones_v(kv))
    o_s = (acc / acc[:, 0:1]).reshape(nh, tq, 128)

    tiles = []
    for r in range(NSA_WTILES):
        start = pl.multiple_of(jnp.maximum(s0 - NSA_WINDOW + r * PREV, 0), PREV)
        tiles.append(kvw_ref[pl.ds(start, PREV), :])
    kvw_b = jnp.concatenate(tiles, axis=0)
    wk = NSA_WKEYS
    kpos_w = lax.broadcasted_iota(jnp.int32, (1, 1, wk), 2) + (s0 - NSA_WINDOW)
    s = _dot_nt(q_b, kvw_b).reshape(nh, tq, wk) + jnp.where(kpos_w >= 0, wb_ref[...], NEG)
    e = jnp.exp2(s - jnp.max(s, axis=2, keepdims=True))
    acc_w = _dot(e.reshape(nh * tq, wk).astype(_bf16), ones_v(kvw_b))
    o_w = (acc_w / acc_w[:, 0:1]).reshape(nh, tq, 128)

    g = jax.nn.sigmoid(gn_ref[...])
    outs = []
    for h in range(nh):
        outs.append(g[:, 3 * h:3 * h + 1] * o_c[h] + g[:, 3 * h + 1:3 * h + 2] * o_s[h]
                    + g[:, 3 * h + 2:3 * h + 3] * o_w[h])
    for hp in range(nh // 2):
        o_ref[:, 128 * hp:128 * hp + 128] = jnp.where(low, pltpu.roll(outs[2 * hp], HEAD_DIM, 1), outs[2 * hp + 1])


def nsa_attention(h, hb, kvc, batch, seq, tables):
    nq = seq // TQ
    nc = seq // NSA_CMP_STRIDE
    cmp_bias, win_bias, near_bias, ov_t = tables
    assert seq // NSA_SEL_BLOCK <= 128
    et = _block_onehot_neg(seq, NSA_SEL_BLOCK)

    def const(shape):
        return pl.BlockSpec(shape, lambda b, c: (0,) * len(shape))

    return pl.pallas_call(
        _nsa_kernel,
        out_shape=jax.ShapeDtypeStruct((batch * seq, NSA_W), _f32),
        grid=(batch, nq),
        in_specs=[pl.BlockSpec((TQ, NSA_W), lambda b, c: (b * nq + c, COL_QN // NSA_W)),
                  pl.BlockSpec((TQ, 128), lambda b, c: (b * nq + c, COL_GN // 128)),
                  pl.BlockSpec((None, nc, 128), lambda b, c: (b, 0, 0)),
                  pl.BlockSpec((seq, 128), lambda b, c: (b, COL_KVS // 128)),
                  pl.BlockSpec((seq, 128), lambda b, c: (b, COL_KVW // 128)),
                  const(et.shape),
                  const(cmp_bias.shape), const(win_bias.shape), const(near_bias.shape), const(ov_t.shape)],
        out_specs=pl.BlockSpec((TQ, NSA_W), lambda b, c: (b * nq + c, 0)),
        compiler_params=_cparams(("parallel", "arbitrary")),
        name="nsa_attention",
    )(h, h, kvc, hb, hb, et, cmp_bias, win_bias, near_bias, ov_t)


def kernel(x, w_in, nsa_cmp_pe, nsa_cmp_w1, nsa_cmp_b1, nsa_cmp_w2, conv_dw_w, conv_dw_b, conv_ln_g,
           conv_ln_b, conv_pw_w, hgrn_lb_logits, hgrn_norm_g, w_out, ln1_g, ln1_b, w_ff1, w_ff2,
           ln2_g, ln2_b, rel_bias):
    batch, seq, d = x.shape
    lb_sm = jax.nn.softmax(hgrn_lb_logits.astype(_f32), axis=0)
    lbs = jnp.cumsum(lb_sm, axis=0) - lb_sm[0]
    x2 = x.reshape(batch * seq, d)
    nsa_tables = _nsa_tables(rel_bias[:NSA_HEADS], seq)
    moba_near = _near_bias(rel_bias[NSA_HEADS:], MOBA_TQ)
    for l in range(DEPTH):
        h, hb = in_proj(x2, _regroup_w_in(w_in[l]))
        o_conv = conformer_conv(h, batch, seq, conv_dw_w[l], conv_dw_b[l], conv_ln_g[l], conv_ln_b[l],
                                conv_pw_w[l].astype(_bf16))
        o_hgrn = hgrn2(h, batch, seq, lbs[l], hgrn_norm_g[l])
        kvc = nsa_compress(h, batch, seq, nsa_cmp_pe[l], nsa_cmp_w1[l], nsa_cmp_b1[l], nsa_cmp_w2[l])
        o_nsa = nsa_attention(h, hb, kvc, batch, seq, nsa_tables)
        o_moba = moba_attention(h, hb, batch, seq, moba_near)
        x1 = out_proj_ln(o_nsa, o_moba, o_conv, o_hgrn, x2, w_out[l].astype(_bf16), ln1_g[l], ln1_b[l])
        x2 = mlp_ln(x1, w_ff1[l].astype(_bf16), w_ff2[l].astype(_bf16), ln2_g[l], ln2_b[l])
    return x2.reshape(batch, seq, d)
```

```python
import math

import jax
import jax.numpy as jnp
import numpy as np
from jax import lax
from jax.experimental import pallas as pl
from jax.experimental.pallas import tpu as pltpu

D_MODEL = 1024
DEPTH = 2
HEAD_DIM = 64
NSA_HEADS = 4
NSA_W = NSA_HEADS * HEAD_DIM
NSA_CMP_LEN = 32
NSA_CMP_STRIDE = 16
NSA_CMP_HIDDEN = 256
NSA_SEL_BLOCK = 64
NSA_TOPN = 16
NSA_WINDOW = 512
MOBA_HEADS = 4
MOBA_W = MOBA_HEADS * HEAD_DIM
MOBA_BLOCK = 256
MOBA_TOPK = 3
CONV_CH = 256
CONV_WIDTH = 31
HGRN_HEADS = 4
HGRN_KDIM = 64
HGRN_VDIM = 64
HGRN_W = HGRN_HEADS * HGRN_VDIM
HGRN_CHUNK = 64
MIX_WIDTH = NSA_W + MOBA_W + CONV_CH + HGRN_W
IN_WIDTHS = (NSA_W, HEAD_DIM, HEAD_DIM, HEAD_DIM, HEAD_DIM, HEAD_DIM, HEAD_DIM, 3 * NSA_HEADS,
             MOBA_W, MOBA_W, MOBA_W, 2 * CONV_CH, HGRN_W, HGRN_W, HGRN_W, HGRN_W)
D_FF = 4 * D_MODEL
REL_BUCKETS = 32
REL_MAX_EXACT = 16
REL_MAX_DIST = 128
TQ = 256
MOBA_TQ = 512
PAIR = 2
PREV = 128
LN_EPS = 1e-5
RMS_EPS = 1e-6
BIG = 1e9
DEEPNORM_ALPHA = (2 * DEPTH) ** 0.25

H_COLS = 3072
COL_QN = 0
COL_KVC = 256
COL_KVS = 384
COL_KVW = 512
COL_GN = 640
COL_QM = 768
COL_KM = 1024
COL_VM = 1280
COL_CONV = 1536
COL_QH = 2048
COL_FH = 2304
COL_IH = 2560
COL_GH = 2816

VMEM_LIMIT = 48 * 1024 * 1024
NEG = -1e30

_f32 = jnp.float32
_bf16 = jnp.bfloat16


def _cparams(sem):
    return pltpu.CompilerParams(dimension_semantics=sem, vmem_limit_bytes=VMEM_LIMIT)


def _layer_norm(z, g, b):
    mu = jnp.mean(z, axis=-1, keepdims=True)
    zc = z - mu
    var = jnp.mean(zc * zc, axis=-1, keepdims=True)
    return zc * lax.rsqrt(var + LN_EPS) * g + b


def _in_proj_kernel(x_ref, w_ref, o_ref, ob_ref):
    acc = jnp.dot(x_ref[...].astype(_bf16), w_ref[...], preferred_element_type=_f32)
    o_ref[...] = acc
    ob_ref[...] = acc.astype(_bf16)


def in_proj(x2, w_bf16, tm=512):
    n, d = x2.shape
    cols = w_bf16.shape[1]
    out = pl.BlockSpec((tm, cols), lambda i: (i, 0))
    return pl.pallas_call(
        _in_proj_kernel,
        out_shape=(jax.ShapeDtypeStruct((n, cols), _f32), jax.ShapeDtypeStruct((n, cols), _bf16)),
        grid=(n // tm,),
        in_specs=[pl.BlockSpec((tm, d), lambda i: (i, 0)),
                  pl.BlockSpec((d, cols), lambda i: (0, 0))],
        out_specs=(out, out),
        compiler_params=_cparams(("parallel",)),
        name="in_proj",
    )(x2, w_bf16)


def _regroup_w_in(w_in_l):
    gate_end = COL_GN + 3 * NSA_HEADS
    src = lax.broadcasted_iota(jnp.int32, (gate_end + H_COLS - COL_QM, H_COLS), 0)
    dst = lax.broadcasted_iota(jnp.int32, (gate_end + H_COLS - COL_QM, H_COLS), 1)
    place = (dst == jnp.where(src < gate_end, src, src + (COL_QM - gate_end))).astype(_bf16)
    assert place.shape[0] == w_in_l.shape[1] == sum(IN_WIDTHS)
    return jnp.dot(w_in_l.astype(_bf16), place, preferred_element_type=_f32).astype(_bf16)


def _out_proj_kernel(a_ref, b_ref, c_ref, d_ref, x_ref, w_ref, g_ref, beta_ref, o_ref):
    mixed = jnp.dot(a_ref[...], w_ref[0:256, :], preferred_element_type=_f32)
    mixed += jnp.dot(b_ref[...], w_ref[256:512, :], preferred_element_type=_f32)
    mixed += jnp.dot(c_ref[...], w_ref[512:768, :], preferred_element_type=_f32)
    mixed += jnp.dot(d_ref[...], w_ref[768:1024, :], preferred_element_type=_f32)
    z = DEEPNORM_ALPHA * x_ref[...] + mixed
    o_ref[...] = _layer_norm(z, g_ref[...], beta_ref[...])


def out_proj_ln(o_nsa, o_moba, o_conv, o_hgrn, x2, w_out_bf16, g, b, tm=512):
    n, d = x2.shape
    part = pl.BlockSpec((tm, 256), lambda i: (i, 0))
    vec = pl.BlockSpec((1, d), lambda i: (0, 0))
    return pl.pallas_call(
        _out_proj_kernel,
        out_shape=jax.ShapeDtypeStruct((n, d), _f32),
        grid=(n // tm,),
        in_specs=[part, part, part, part,
                  pl.BlockSpec((tm, d), lambda i: (i, 0)),
                  pl.BlockSpec((MIX_WIDTH, d), lambda i: (0, 0)),
                  vec, vec],
        out_specs=pl.BlockSpec((tm, d), lambda i: (i, 0)),
        compiler_params=_cparams(("parallel",)),
        name="out_proj_ln",
    )(o_nsa, o_moba, o_conv, o_hgrn, x2, w_out_bf16, g.reshape(1, d), b.reshape(1, d))


def _mlp_kernel(x_ref, w1_ref, w2_ref, g_ref, beta_ref, o_ref, acc_ref):
    j = pl.program_id(1)

    @pl.when(j == 0)
    def _():
        acc_ref[...] = jnp.zeros_like(acc_ref)

    hid = jnp.dot(x_ref[...].astype(_bf16), w1_ref[...], preferred_element_type=_f32)
    hid = jnp.square(jnp.maximum(hid, 0.0))
    acc_ref[...] += jnp.dot(hid.astype(_bf16), w2_ref[...], preferred_element_type=_f32)

    @pl.when(j == pl.num_programs(1) - 1)
    def _():
        z = DEEPNORM_ALPHA * x_ref[...] + acc_ref[...]
        o_ref[...] = _layer_norm(z, g_ref[...], beta_ref[...])


def mlp_ln(x1, w1_bf16, w2_bf16, g, b, tm=1024, tf=1024):
    n, d = x1.shape
    dff = w1_bf16.shape[1]
    vec = pl.BlockSpec((1, d), lambda i, j: (0, 0))
    return pl.pallas_call(
        _mlp_kernel,
        out_shape=jax.ShapeDtypeStruct((n, d), _f32),
        grid=(n // tm, dff // tf),
        in_specs=[pl.BlockSpec((tm, d), lambda i, j: (i, 0)),
                  pl.BlockSpec((d, tf), lambda i, j: (0, j)),
                  pl.BlockSpec((tf, d), lambda i, j: (j, 0)),
                  vec, vec],
        out_specs=pl.BlockSpec((tm, d), lambda i, j: (i, 0)),
        scratch_shapes=[pltpu.VMEM((tm, d), _f32)],
        compiler_params=_cparams(("parallel", "arbitrary")),
        name="mlp_ln",
    )(x1, w1_bf16, w2_bf16, g.reshape(1, d), b.reshape(1, d))


SUBLANES = 8
CONV_HALO = 32


def _conv_kernel(u_ref, wdw_ref, bdw_ref, g_ref, beta_ref, wpw_ref, o_ref, buf_ref, rot_ref):
    t = pl.program_id(1)
    tt = u_ref.shape[0]

    @pl.when(t == 0)
    def _():
        buf_ref[0:CONV_HALO, :] = jnp.zeros((CONV_HALO, CONV_CH), _f32)
        buf_ref[CONV_HALO + tt:, :] = jnp.zeros((SUBLANES, CONV_CH), _f32)

    @pl.when(t > 0)
    def _():
        buf_ref[0:CONV_HALO, :] = buf_ref[tt:tt + CONV_HALO, :]

    u = u_ref[...]
    a = u[:, :CONV_CH]
    gl = u[:, CONV_CH:]
    buf_ref[CONV_HALO:CONV_HALO + tt, :] = a * jax.nn.sigmoid(gl)

    acc = jnp.zeros((tt, CONV_CH), _f32) + bdw_ref[...]
    first = CONV_HALO - (CONV_WIDTH - 1)
    for r in range(SUBLANES):
        taps = [w for w in range(CONV_WIDTH) if (first + w) % SUBLANES == r]
        if not taps:
            continue
        rot_ref[r % 2] = buf_ref[r:r + CONV_HALO + tt, :]
        for w in taps:
            base = first + w - r
            acc = acc + rot_ref[r % 2, base:base + tt, :] * wdw_ref[w:w + 1, :]
    hn = _layer_norm(acc, g_ref[...], beta_ref[...])
    hn = hn * jax.nn.sigmoid(hn)
    o_ref[...] = jnp.dot(hn.astype(_bf16), wpw_ref[...], preferred_element_type=_f32).astype(o_ref.dtype)


def conformer_conv(h, batch, seq, w_dw, b_dw, ln_g, ln_b, w_pw_bf16, tt=512):
    nt = seq // tt
    vec = pl.BlockSpec((1, CONV_CH), lambda b, t: (0, 0))
    return pl.pallas_call(
        _conv_kernel,
        out_shape=jax.ShapeDtypeStruct((batch * seq, CONV_CH), _bf16),
        grid=(batch, nt),
        in_specs=[pl.BlockSpec((tt, 2 * CONV_CH), lambda b, t: (b * nt + t, COL_CONV // (2 * CONV_CH))),
                  pl.BlockSpec((CONV_WIDTH, CONV_CH), lambda b, t: (0, 0)),
                  vec, vec, vec,
                  pl.BlockSpec((CONV_CH, CONV_CH), lambda b, t: (0, 0))],
        out_specs=pl.BlockSpec((tt, CONV_CH), lambda b, t: (b * nt + t, 0)),
        scratch_shapes=[pltpu.VMEM((CONV_HALO + tt + SUBLANES, CONV_CH), _f32),
                        pltpu.VMEM((2, CONV_HALO + tt, CONV_CH), _f32)],
        compiler_params=_cparams(("parallel", "arbitrary")),
        name="conformer_conv",
    )(h, w_dw, b_dw.reshape(1, CONV_CH), ln_g.reshape(1, CONV_CH), ln_b.reshape(1, CONV_CH), w_pw_bf16)


HGRN_LEVELS = (32, 16, 8, 4, 2, 1)
HGRN_TILE = 256


def _hgrn_constants():
    c = HGRN_CHUNK
    t = np.arange(c)[:, None]
    u = np.arange(c)[None, :]
    masks = []
    for h in HGRN_LEVELS:
        mid = (t // (2 * h)) * (2 * h) + h
        upper = t >= mid
        same = (t // (2 * h)) == (u // (2 * h))
        mid_s = (u // (2 * h)) * (2 * h) + h
        masks.append(same & upper & (u < mid_s))
    masks.append(t == u)
    lmat = (u <= t).astype(np.float32)
    mask = np.stack(masks).astype(np.float32)
    mask = np.tile(mask, (1, HGRN_HEADS, 1))
    lane_head = np.arange(HGRN_W)[None, :] // HGRN_KDIM
    lane_mask = (lane_head == np.arange(HGRN_HEADS)[:, None]).astype(np.float32)
    bd = (lane_head.T == lane_head).astype(np.float32)
    return lmat, mask, lane_mask, bd


def _split3(x):
    hi = x.astype(_bf16)
    r = x - hi.astype(_f32)
    mid = r.astype(_bf16)
    lo = (r - mid.astype(_f32)).astype(_bf16)
    return hi, mid, lo


def _dot(a, b):
    return jnp.dot(a, b, preferred_element_type=_f32)


def _dot_nt(a, b):
    return lax.dot_general(a, b, (((1,), (1,)), ((), ())), preferred_element_type=_f32)


def _dot_tn(a, b):
    return lax.dot_general(a, b, (((0,), (0,)), ((), ())), preferred_element_type=_f32)


def _dot3(l_bf16, x):
    hi, mid, lo = _split3(x)
    return _dot(l_bf16, hi) + _dot(l_bf16, mid) + _dot(l_bf16, lo)


def _level_exponent(h, cum, logf, prev_g, next_g, row):
    c, w = cum.shape
    if h == 1:
        return jnp.where(row % 2 == 1, logf, 0.0)
    if h == 2:
        ph = row % 4
        return jnp.where(ph == 2, logf, jnp.where(ph == 3, logf + prev_g, jnp.where(ph == 0, next_g, 0.0)))
    pieces = [jnp.broadcast_to(cum[b * 2 * h + h - 1:b * 2 * h + h, :], (2 * h, w)) for b in range(c // (2 * h))]
    ref = pieces[0] if len(pieces) == 1 else jnp.concatenate(pieces, axis=0)
    return -jnp.abs(cum - ref)


def _hgrn_kernel(q_ref, f_ref, i_ref, g_ref, lb_ref, ng_ref, lmat_ref, mask_ref, lm_ref, bd_ref,
                 o_ref, st_ref):
    c = HGRN_CHUNK
    nl = len(HGRN_LEVELS)

    @pl.when(pl.program_id(1) == 0)
    def _():
        st_ref[...] = jnp.zeros_like(st_ref)

    lb = lb_ref[...]
    bd = bd_ref[...]
    row = lax.broadcasted_iota(jnp.int32, (c, 1), 0)

    def chunk(ci):
        rows = pl.ds(ci * c, c)
        q = q_ref[rows, :]
        f = lb + (1.0 - lb) * jax.nn.sigmoid(f_ref[rows, :])
        logf = jnp.log(f)
        k = 1.0 - f
        v = i_ref[rows, :]
        v_b = v.astype(_bf16)
        cum = _dot3(lmat_ref[...], logf)
        e_cum = jnp.exp(cum)
        e_rest = jnp.exp(cum[c - 1:c, :] - cum)
        prev_g = pltpu.roll(logf, 1, 0)
        next_g = pltpu.roll(logf, c - 1, 0)

        attn = jnp.zeros((HGRN_HEADS * c, c), _f32)
        for l in range(nl + 1):
            if l < nl:
                el = jnp.exp(_level_exponent(HGRN_LEVELS[l], cum, logf, prev_g, next_g, row))
                ql, kl = q * el, k * el
            else:
                ql, kl = q, k
            qs = jnp.concatenate([ql * lm_ref[h:h + 1, :] for h in range(HGRN_HEADS)], axis=0)
            attn = attn + _dot_nt(qs.astype(_bf16), kl.astype(_bf16)) * mask_ref[l]
        o = jnp.zeros((c, HGRN_W), _f32)
        for h in range(HGRN_HEADS):
            vh = (v * lm_ref[h:h + 1, :]).astype(_bf16)
            o = o + _dot(attn[h * c:(h + 1) * c, :].astype(_bf16), vh)

        st = st_ref[...]
        o = o + _dot_nt((q * e_cum).astype(_bf16), st.astype(_bf16))
        upd = _dot_tn(v_b, (k * e_rest).astype(_bf16)) * bd
        st_ref[...] = st * e_cum[c - 1:c, :] + upd

        osq = o * o
        hi = osq.astype(_bf16)
        lo = (osq - hi.astype(_f32)).astype(_bf16)
        bd_b = bd.astype(_bf16)
        ms = (_dot(hi, bd_b) + _dot(lo, bd_b)) * (1.0 / HGRN_VDIM)
        o = o * lax.rsqrt(ms + RMS_EPS)
        o_ref[rows, :] = (o * ng_ref[...] * jax.nn.sigmoid(g_ref[rows, :])).astype(o_ref.dtype)

    for ci in range(q_ref.shape[0] // c):
        chunk(ci)


def hgrn2(h, batch, seq, lb, norm_g):
    lmat, mask, lane_mask, bd = _hgrn_constants()
    tt = HGRN_TILE
    nt = seq // tt

    def col(cb):
        return pl.BlockSpec((tt, HGRN_W), lambda b, t: (b * nt + t, cb))

    def const(shape):
        nd = len(shape)
        return pl.BlockSpec(shape, lambda b, t: (0,) * nd)

    return pl.pallas_call(
        _hgrn_kernel,
        out_shape=jax.ShapeDtypeStruct((batch * seq, HGRN_W), _bf16),
        grid=(batch, nt),
        in_specs=[col(COL_QH // HGRN_W), col(COL_FH // HGRN_W), col(COL_IH // HGRN_W), col(COL_GH // HGRN_W),
                  const((1, HGRN_W)), const((1, HGRN_W)),
                  const(lmat.shape), const(mask.shape), const(lane_mask.shape), const(bd.shape)],
        out_specs=pl.BlockSpec((tt, HGRN_W), lambda b, t: (b * nt + t, 0)),
        scratch_shapes=[pltpu.VMEM((HGRN_W, HGRN_W), _f32)],
        compiler_params=_cparams(("parallel", "arbitrary")),
        name="hgrn2",
    )(h, h, h, h, lb.reshape(1, HGRN_W), norm_g.reshape(1, HGRN_W),
      jnp.asarray(lmat, _bf16), jnp.asarray(mask), jnp.asarray(lane_mask), jnp.asarray(bd))


def _rel_bucket_np(dist):
    n = np.maximum(dist, 0)
    nf = np.maximum(n, 1).astype(np.float32)
    large = REL_MAX_EXACT + (np.log(nf / np.float32(REL_MAX_EXACT)) / np.float32(math.log(REL_MAX_DIST / REL_MAX_EXACT))
                             * np.float32(REL_BUCKETS - REL_MAX_EXACT)).astype(np.int32)
    return np.where(n < REL_MAX_EXACT, n, np.minimum(large, REL_BUCKETS - 1)).astype(np.int32)


FAR_DIST = 113
assert int(_rel_bucket_np(np.arange(FAR_DIST, 4 * FAR_DIST)).min()) == REL_BUCKETS - 1


LOG2E = 1.4426950408889634
M_FLOOR = -5e29


def _bias_rel(tab, dist):
    dist = np.asarray(dist)
    vals = jnp.take(tab, jnp.asarray(_rel_bucket_np(dist).reshape(-1)), axis=1)
    vals = (vals - tab[:, REL_BUCKETS - 1:]) * LOG2E
    vals = vals.reshape((tab.shape[0],) + dist.shape)
    return jnp.where(jnp.asarray(dist >= 0)[None], vals, NEG)


def _toeplitz_kernel(g_ref, o_ref):
    rows, width = o_ref.shape
    x = jnp.broadcast_to(g_ref[...], (rows, g_ref.shape[1]))
    o_ref[...] = pltpu.roll(x, 0, 1, stride=1, stride_axis=0)[:, :width]


def _toeplitz(g, rows, width):
    nh, length = g.shape
    return pl.pallas_call(
        _toeplitz_kernel,
        out_shape=jax.ShapeDtypeStruct((nh, rows, width), _f32),
        grid=(nh,),
        in_specs=[pl.BlockSpec((None, 1, length), lambda h: (h, 0, 0))],
        out_specs=pl.BlockSpec((None, rows, width), lambda h: (h, 0, 0)),
        compiler_params=_cparams(("parallel",)),
        name="toeplitz_bias",
    )(g.reshape(nh, 1, length))


def _near_bias(tab, tq):
    k = np.arange(2 * (PREV + tq))
    g = _bias_rel(tab, np.where(k < PREV + tq, PREV - k, 4 * FAR_DIST))
    return _toeplitz(g, tq, PREV + tq)


FAR_TILE = 512
FAR_UNROLL = 4
MOBA_UNROLL = 4


def _softmax_update(m, acc, s, rhs_b):
    m_new = jnp.maximum(m, jnp.max(s, axis=-1, keepdims=True))
    alpha = jnp.exp2(m - m_new)
    p = jnp.exp2(s - m_new)
    return m_new, alpha * acc + _dot(p.astype(_bf16), rhs_b)


def _far_sweep(n_tiles, tile_step, init, unroll, exact_tail):
    def body(it, carry):
        for u in range(unroll):
            carry = tile_step(it * unroll + u, carry)
        return carry

    if not exact_tail:
        return lax.fori_loop(0, (n_tiles + unroll - 1) // unroll, body, init)
    n_main = n_tiles // unroll
    carry = lax.fori_loop(0, n_main, body, init)
    return lax.fori_loop(n_main * unroll, n_tiles, tile_step, carry)


def _block_onehot_neg(seq, block):
    k = lax.broadcasted_iota(jnp.int32, (seq, 128), 0)
    lane = lax.broadcasted_iota(jnp.int32, (seq, 128), 1)
    return jnp.where(lane == k // block, NEG, 0.0).astype(_bf16)


MOBA_UNIT = PREV


def _moba_kernel(q_ref, k_ref, v_ref, et_ref, bn_ref, o_ref, kmean_ref):
    c = pl.program_id(2)
    seq = k_ref.shape[1]
    nb = seq // MOBA_BLOCK
    tq = MOBA_TQ

    @pl.when(c == 0)
    def _():
        for g in range(PAIR):
            kmean_ref[g] = jnp.sum(k_ref[g].astype(_f32).reshape(nb, MOBA_BLOCK, 128), axis=1) * (1.0 / MOBA_BLOCK)

    s0 = c * tq
    far_end = s0 - PREV
    n_far = (jnp.maximum(far_end, 0) + FAR_TILE - 1) // FAR_TILE

    lane = lax.broadcasted_iota(jnp.int32, (1, 128), 1)
    jcol = lax.broadcasted_iota(jnp.int32, (nb, tq), 0)
    own = (lax.broadcasted_iota(jnp.int32, (nb, tq), 1) + s0) // MOBA_BLOCK
    rep = (lax.broadcasted_iota(jnp.int32, (nb, 128), 1) // (MOBA_BLOCK // MOBA_UNIT)
           == lax.broadcasted_iota(jnp.int32, (nb, 128), 0)).astype(_bf16)

    qfs, qns = [], []
    for g in range(PAIR):
        q_all = q_ref[g] * (HEAD_DIM ** -0.5 * LOG2E)
        kmean_b = kmean_ref[g].astype(_bf16)
        q_far, q_near = [], []
        for hh in range(2):
            qh = jnp.where(lane // HEAD_DIM == hh, q_all, 0.0).astype(_bf16)
            gate = _dot_nt(kmean_b, qh)
            work = jnp.where(jcol < own, gate, -BIG)
            sel = jnp.zeros((nb, tq), _f32)
            for _ in range(MOBA_TOPK):
                mx = jnp.max(work, axis=0, keepdims=True)
                idx = jnp.min(jnp.where(work == mx, jcol, nb), axis=0, keepdims=True)
                pick = jcol == idx
                sel = jnp.where(pick & (mx > -0.5 * BIG * LOG2E), 1.0, sel)
                work = jnp.where(pick, -jnp.inf, work)
            sel = jnp.where(jcol == own, 1.0, sel)
            notsel = _dot_tn((1.0 - sel).astype(_bf16), rep)
            notsel_far = jnp.where(lane < far_end // MOBA_UNIT, notsel, 1.0)
            q_far.append(jnp.concatenate([qh, notsel_far.astype(_bf16)], axis=1))
            q_near.append(jnp.concatenate([qh, notsel.astype(_bf16)], axis=1))
        qfs.append(jnp.concatenate(q_far, axis=0))
        qns.append(jnp.concatenate(q_near, axis=0))
    ones = jnp.ones((max(FAR_TILE, PREV + tq), 128), _bf16)

    def far_tile(kt, carry):
        rows = pl.ds(pl.multiple_of(kt * FAR_TILE, FAR_TILE), FAR_TILE)
        et = et_ref[rows, :]
        out = []
        for g in range(PAIR):
            s = _dot_nt(qfs[g], jnp.concatenate([k_ref[g, rows, :], et], axis=1))
            out.append(_softmax_update(*carry[g], s, jnp.concatenate([v_ref[g, rows, :], ones[:FAR_TILE]], axis=1)))
        return tuple(out)

    init = tuple((jnp.full((2 * tq, 1), M_FLOOR, _f32), jnp.zeros((2 * tq, 256), _f32)) for _ in range(PAIR))
    state = _far_sweep(n_far, far_tile, init, MOBA_UNROLL, exact_tail=False)

    prev = pl.ds(pl.multiple_of(jnp.maximum(far_end, 0), PREV), PREV)
    diag = pl.ds(pl.multiple_of(s0, tq), tq)
    col = lax.broadcasted_iota(jnp.int32, (1, 1, PREV + tq), 2)
    nbias = jnp.where((col < PREV) & (c == 0), NEG, bn_ref[...])
    nbias = nbias.reshape(2 * tq, PREV + tq)
    et_n = jnp.concatenate([et_ref[prev, :], et_ref[diag, :]], axis=0)
    for g in range(PAIR):
        k_n = jnp.concatenate([jnp.concatenate([k_ref[g, prev, :], k_ref[g, diag, :]], axis=0), et_n], axis=1)
        v_n = jnp.concatenate([jnp.concatenate([v_ref[g, prev, :], v_ref[g, diag, :]], axis=0), ones[:PREV + tq]],
                              axis=1)
        _, acc = _softmax_update(*state[g], _dot_nt(qns[g], k_n) + nbias, v_n)
        out = acc[:, :128] / acc[:, 128:]
        o_ref[g] = jnp.where(lane // HEAD_DIM == 0, out[:tq], out[tq:]).astype(o_ref.dtype)


def moba_attention(h, hb, batch, seq, bias_near):
    tq = MOBA_TQ
    nq = seq // tq
    assert seq // MOBA_UNIT <= 128 and tq % MOBA_BLOCK == 0 and batch % PAIR == 0
    et = _block_onehot_neg(seq, MOBA_UNIT)
    h3 = h.reshape(batch, seq, H_COLS)
    hb3 = hb.reshape(batch, seq, H_COLS)

    def kv(col):
        return pl.BlockSpec((PAIR, seq, 128), lambda b, hp, c: (b, 0, col // 128 + hp))

    out = pl.pallas_call(
        _moba_kernel,
        out_shape=jax.ShapeDtypeStruct((batch, seq, MOBA_W), _bf16),
        grid=(batch // PAIR, MOBA_HEADS // 2, nq),
        in_specs=[pl.BlockSpec((PAIR, tq, 128), lambda b, hp, c: (b, c, COL_QM // 128 + hp)),
                  kv(COL_KM), kv(COL_VM),
                  pl.BlockSpec((seq, 128), lambda b, hp, c: (0, 0)),
                  pl.BlockSpec((2, tq, PREV + tq), lambda b, hp, c: (hp, 0, 0))],
        out_specs=pl.BlockSpec((PAIR, tq, 128), lambda b, hp, c: (b, c, hp)),
        scratch_shapes=[pltpu.VMEM((PAIR, seq // MOBA_BLOCK, 128), _f32)],
        compiler_params=_cparams(("parallel", "parallel", "arbitrary")),
        name="moba_attention",
    )(h3, hb3, hb3, et, bias_near)
    return out.reshape(batch * seq, MOBA_W)


def _compress_kernel(a_ref, pet_ref, peb_ref, wt_ref, wb_ref, b1_ref, w2_ref, o_ref):
    a = a_ref[...]
    nrow = a.shape[0]
    top = _dot((a + pet_ref[...]).astype(_bf16), wt_ref[...])
    bot = _dot((a + peb_ref[...]).astype(_bf16), wb_ref[...])
    pre = top + pltpu.roll(bot, nrow - 1, 0) + b1_ref[...]
    hid = jax.nn.gelu(pre)
    o_ref[...] = _dot(hid.astype(_bf16), w2_ref[...])


def nsa_compress(h, batch, seq, pe, w1, b1, w2):
    nr = seq // NSA_CMP_STRIDE
    half = NSA_CMP_LEN // 2
    a = h[:, COL_KVC:COL_KVC + 128].reshape(batch, nr, NSA_CMP_STRIDE * 128)
    pe_cat = jnp.concatenate([pe[0], pe[1]], axis=-1)
    pe_top = pe_cat[:half].reshape(1, half * 128)
    pe_bot = pe_cat[half:].reshape(1, half * 128)
    hid = NSA_CMP_HIDDEN
    w_all = jnp.zeros((NSA_CMP_LEN, 128, 2 * hid), _f32)
    w_all = w_all.at[:, :HEAD_DIM, :hid].set(w1[0].reshape(NSA_CMP_LEN, HEAD_DIM, hid))
    w_all = w_all.at[:, HEAD_DIM:, hid:].set(w1[1].reshape(NSA_CMP_LEN, HEAD_DIM, hid))
    w_top = w_all[:half].reshape(half * 128, 2 * hid).astype(_bf16)
    w_bot = w_all[half:].reshape(half * 128, 2 * hid).astype(_bf16)
    b1cat = jnp.concatenate([b1[0], b1[1]]).reshape(1, 2 * hid)
    w2bd = jnp.zeros((2 * hid, 128), _f32)
    w2bd = w2bd.at[:hid, :HEAD_DIM].set(w2[0]).at[hid:, HEAD_DIM:].set(w2[1]).astype(_bf16)

    def const(shape):
        return pl.BlockSpec(shape, lambda b: (0,) * len(shape))

    return pl.pallas_call(
        _compress_kernel,
        out_shape=jax.ShapeDtypeStruct((batch, nr, 128), _f32),
        grid=(batch,),
        in_specs=[pl.BlockSpec((None, nr, half * 128), lambda b: (b, 0, 0)),
                  const(pe_top.shape), const(pe_bot.shape), const(w_top.shape), const(w_bot.shape),
                  const(b1cat.shape), const(w2bd.shape)],
        out_specs=pl.BlockSpec((None, nr, 128), lambda b: (b, 0, 0)),
        compiler_params=_cparams(("parallel",)),
        name="nsa_compress",
    )(a, pe_top, pe_bot, w_top, w_bot, b1cat, w2bd)


NSA_FORCED = 3
assert NSA_TOPN >= NSA_FORCED
CMP_FIRST = PREV // NSA_CMP_STRIDE
CMP_NEAR = -(-((TQ + PREV - NSA_CMP_LEN) // NSA_CMP_STRIDE + 1) // 8) * 8
NSA_WKEYS = NSA_WINDOW + TQ
NSA_WTILES = NSA_WKEYS // PREV


def _nsa_tables(tab, seq):
    nc = seq // NSA_CMP_STRIDE
    nsel = seq // NSA_SEL_BLOCK
    n_cmp = (seq - NSA_CMP_LEN) // NSA_CMP_STRIDE + 1
    i = np.arange(TQ)[:, None]
    m = np.arange(CMP_NEAR)[None, :]
    d_c = i + (PREV - NSA_CMP_LEN + 1) - NSA_CMP_STRIDE * m
    cmp_bias = _bias_rel(tab, np.maximum(d_c, 0))
    k = np.arange(2 * NSA_WKEYS)
    g = _bias_rel(tab, np.where((k >= 1) & (k <= NSA_WINDOW), NSA_WINDOW - k, -1))
    win_bias = _toeplitz(g, TQ, NSA_WKEYS)
    n = np.arange(nc)[:, None]
    js = np.arange(nsel)[None, :]
    cs, ce = n * NSA_CMP_STRIDE, n * NSA_CMP_STRIDE + NSA_CMP_LEN
    ss = js * NSA_SEL_BLOCK
    ov = np.clip(np.minimum(ce, ss + NSA_SEL_BLOCK) - np.maximum(cs, ss), 0, None).astype(np.float32) / NSA_CMP_LEN
    ov = np.where(n < n_cmp, ov, 0.0)
    return cmp_bias, win_bias, _near_bias(tab, TQ), jnp.asarray(ov.T, _bf16)


def _exact_dot_r(x, r_bf16):
    hi = x.astype(_bf16)
    lo = (x - hi.astype(_f32)).astype(_bf16)
    return _dot(hi, r_bf16) + _dot(lo, r_bf16)


def _nsa_kernel(q_ref, gn_ref, kvc_ref, kvs_ref, kvw_ref, et_ref, cb_ref, wb_ref, nb_ref, ovt_ref, o_ref):
    c = pl.program_id(1)
    tq = TQ
    nh = NSA_HEADS
    nc = kvc_ref.shape[0]
    nsel = ovt_ref.shape[0]
    s0 = c * tq
    far_end = s0 - PREV
    lane = lax.broadcasted_iota(jnp.int32, (1, 128), 1)
    low = lane < HEAD_DIM

    q_all = q_ref[...] * (HEAD_DIM ** -0.5 * LOG2E)
    qh = []
    for h in range(nh):
        blk = q_all[:, 128 * (h // 2):128 * (h // 2) + 128]
        if h % 2 == 1:
            blk = pltpu.roll(blk, HEAD_DIM, 1)
        qh.append(jnp.where(low, blk, 0.0))
    q_b = jnp.concatenate(qh, axis=0).astype(_bf16)

    def ones_v(kv):
        return jnp.where(low, 1.0, kv).astype(_bf16)

    kvc_b = kvc_ref[...].astype(_bf16)
    s_c = _dot_nt(q_b, kvc_b).reshape(nh, tq, nc)
    n_idx = lax.broadcasted_iota(jnp.int32, (tq, nc), 1)
    t_idx = lax.broadcasted_iota(jnp.int32, (tq, nc), 0) + s0
    mask_c = jnp.where((n_idx * NSA_CMP_STRIDE + (NSA_CMP_LEN - 1)) <= t_idx, 0.0, NEG)
    m_sel = lax.broadcasted_iota(jnp.int32, (CMP_NEAR, nc), 0)
    n_sel = lax.broadcasted_iota(jnp.int32, (CMP_NEAR, nc), 1)
    place = (n_sel == (s0 // NSA_CMP_STRIDE) - CMP_FIRST + m_sel).astype(_bf16)
    bias_c = _exact_dot_r(cb_ref[...].reshape(nh * tq, CMP_NEAR), place).reshape(nh, tq, nc)
    psum = jnp.zeros((tq, nc), _f32)
    o_c = []
    for h in range(nh):
        s = s_c[h] + bias_c[h] + mask_c
        mx = jnp.maximum(jnp.max(s, axis=1, keepdims=True), M_FLOOR)
        e = jnp.exp2(s - mx)
        p = e * (1.0 / jnp.maximum(jnp.sum(e, axis=1, keepdims=True), 1e-30))
        psum = psum + p
        o_c.append(_dot(p.astype(_bf16), kvc_b))

    imp_t = jnp.zeros((nsel, tq), _f32)
    for part in _split3(psum):
        imp_t = imp_t + _dot_nt(ovt_ref[...], part)
    jblk = lax.broadcasted_iota(jnp.int32, (nsel, tq), 0)
    blk_t = (lax.broadcasted_iota(jnp.int32, (nsel, tq), 1) + s0) // NSA_SEL_BLOCK
    forced = (jblk == 0) | (jblk == blk_t) | (jblk == blk_t - 1)
    work = jnp.where((jblk <= blk_t) & jnp.logical_not(forced), imp_t, -BIG)
    for _ in range(min(NSA_TOPN, nsel) - NSA_FORCED):
        mx = jnp.max(work, axis=0, keepdims=True)
        idx = jnp.min(jnp.where(work == mx, jblk, nsel), axis=0, keepdims=True)
        work = jnp.where(jblk == idx, -jnp.inf, work)
    sel = jnp.where(forced | (work == -jnp.inf), 1.0, 0.0)

    eye = (lax.broadcasted_iota(jnp.int32, (nsel, 128), 0)
           == lax.broadcasted_iota(jnp.int32, (nsel, 128), 1)).astype(_bf16)
    notsel = _dot_tn((1.0 - sel).astype(_bf16), eye)
    notsel_far = jnp.where(lane < far_end // NSA_SEL_BLOCK, notsel, 1.0)
    qf = jnp.concatenate([q_b, jnp.concatenate([notsel_far.astype(_bf16)] * nh, axis=0)], axis=1)
    qn = jnp.concatenate([q_b, jnp.concatenate([notsel.astype(_bf16)] * nh, axis=0)], axis=1)

    def far_tile(kt, carry):
        rows = pl.ds(pl.multiple_of(kt * FAR_TILE, FAR_TILE), FAR_TILE)
        kv = kvs_ref[rows, :]
        s = _dot_nt(qf, jnp.concatenate([kv, et_ref[rows, :]], axis=1))
        return _softmax_update(*carry, s, ones_v(kv))

    n_far = (jnp.maximum(far_end, 0) + FAR_TILE - 1) // FAR_TILE
    init = (jnp.full((nh * tq, 1), M_FLOOR, _f32), jnp.zeros((nh * tq, 128), _f32))
    m, acc = _far_sweep(n_far, far_tile, init, FAR_UNROLL, exact_tail=True)

    prev = pl.ds(pl.multiple_of(jnp.maximum(far_end, 0), PREV), PREV)
    diag = pl.ds(pl.multiple_of(s0, tq), tq)
    kv = jnp.concatenate([kvs_ref[prev, :], kvs_ref[diag, :]], axis=0)
    et = jnp.concatenate([et_ref[prev, :], et_ref[diag, :]], axis=0)
    col = lax.broadcasted_iota(jnp.int32, (1, 1, PREV + tq), 2)
    nbias = jnp.where((col < PREV) & (c == 0), NEG, nb_ref[...])
    s = _dot_nt(qn, jnp.concatenate([kv, et], axis=1)) + nbias.reshape(nh * tq, PREV + tq)
    _, acc = _softmax_update(m, acc, s, ones_v(kv))
    o_s = (acc / acc[:, 0:1]).reshape(nh, tq, 128)

    tiles = []
    for r in range(NSA_WTILES):
        start = pl.multiple_of(jnp.maximum(s0 - NSA_WINDOW + r * PREV, 0), PREV)
        tiles.append(kvw_ref[pl.ds(start, PREV), :])
    kvw_b = jnp.concatenate(tiles, axis=0)
    wk = NSA_WKEYS
    kpos_w = lax.broadcasted_iota(jnp.int32, (1, 1, wk), 2) + (s0 - NSA_WINDOW)
    s = _dot_nt(q_b, kvw_b).reshape(nh, tq, wk) + jnp.where(kpos_w >= 0, wb_ref[...], NEG)
    e = jnp.exp2(s - jnp.max(s, axis=2, keepdims=True))
    acc_w = _dot(e.reshape(nh * tq, wk).astype(_bf16), ones_v(kvw_b))
    o_w = (acc_w / acc_w[:, 0:1]).reshape(nh, tq, 128)

    g = jax.nn.sigmoid(gn_ref[...])
    outs = []
    for h in range(nh):
        outs.append(g[:, 3 * h:3 * h + 1] * o_c[h] + g[:, 3 * h + 1:3 * h + 2] * o_s[h]
                    + g[:, 3 * h + 2:3 * h + 3] * o_w[h])
    for hp in range(nh // 2):
        o_ref[:, 128 * hp:128 * hp + 128] = jnp.where(low, pltpu.roll(outs[2 * hp], HEAD_DIM, 1),
                                                      outs[2 * hp + 1]).astype(o_ref.dtype)


def nsa_attention(h, hb, kvc, batch, seq, tables):
    nq = seq // TQ
    nc = seq // NSA_CMP_STRIDE
    cmp_bias, win_bias, near_bias, ov_t = tables
    assert seq // NSA_SEL_BLOCK <= 128
    et = _block_onehot_neg(seq, NSA_SEL_BLOCK)

    def const(shape):
        return pl.BlockSpec(shape, lambda b, c: (0,) * len(shape))

    return pl.pallas_call(
        _nsa_kernel,
        out_shape=jax.ShapeDtypeStruct((batch * seq, NSA_W), _bf16),
        grid=(batch, nq),
        in_specs=[pl.BlockSpec((TQ, NSA_W), lambda b, c: (b * nq + c, COL_QN // NSA_W)),
                  pl.BlockSpec((TQ, 128), lambda b, c: (b * nq + c, COL_GN // 128)),
                  pl.BlockSpec((None, nc, 128), lambda b, c: (b, 0, 0)),
                  pl.BlockSpec((seq, 128), lambda b, c: (b, COL_KVS // 128)),
                  pl.BlockSpec((seq, 128), lambda b, c: (b, COL_KVW // 128)),
                  const(et.shape),
                  const(cmp_bias.shape), const(win_bias.shape), const(near_bias.shape), const(ov_t.shape)],
        out_specs=pl.BlockSpec((TQ, NSA_W), lambda b, c: (b * nq + c, 0)),
        compiler_params=_cparams(("parallel", "arbitrary")),
        name="nsa_attention",
    )(h, h, kvc, hb, hb, et, cmp_bias, win_bias, near_bias, ov_t)


def kernel(x, w_in, nsa_cmp_pe, nsa_cmp_w1, nsa_cmp_b1, nsa_cmp_w2, conv_dw_w, conv_dw_b, conv_ln_g,
           conv_ln_b, conv_pw_w, hgrn_lb_logits, hgrn_norm_g, w_out, ln1_g, ln1_b, w_ff1, w_ff2,
           ln2_g, ln2_b, rel_bias):
    batch, seq, d = x.shape
    lb_sm = jax.nn.softmax(hgrn_lb_logits.astype(_f32), axis=0)
    lbs = jnp.cumsum(lb_sm, axis=0) - lb_sm[0]
    x2 = x.reshape(batch * seq, d)
    nsa_tables = _nsa_tables(rel_bias[:NSA_HEADS], seq)
    moba_near = _near_bias(rel_bias[NSA_HEADS:], MOBA_TQ)
    for l in range(DEPTH):
        h, hb = in_proj(x2, _regroup_w_in(w_in[l]))
        o_conv = conformer_conv(h, batch, seq, conv_dw_w[l], conv_dw_b[l], conv_ln_g[l], conv_ln_b[l],
                                conv_pw_w[l].astype(_bf16))
        o_hgrn = hgrn2(h, batch, seq, lbs[l], hgrn_norm_g[l])
        kvc = nsa_compress(h, batch, seq, nsa_cmp_pe[l], nsa_cmp_w1[l], nsa_cmp_b1[l], nsa_cmp_w2[l])
        o_nsa = nsa_attention(h, hb, kvc, batch, seq, nsa_tables)
        o_moba = moba_attention(h, hb, batch, seq, moba_near)
        x1 = out_proj_ln(o_nsa, o_moba, o_conv, o_hgrn, x2, w_out[l].astype(_bf16), ln1_g[l], ln1_b[l])
        x2 = mlp_ln(x1, w_ff1[l].astype(_bf16), w_ff2[l].astype(_bf16), ln2_g[l], ln2_b[l])
    return x2.reshape(batch, seq, d)
```

```python
import math

import jax
import jax.numpy as jnp
import numpy as np
from jax import lax
from jax.experimental import pallas as pl
from jax.experimental.pallas import tpu as pltpu

D_MODEL = 1024
DEPTH = 2
HEAD_DIM = 64
NSA_HEADS = 4
NSA_W = NSA_HEADS * HEAD_DIM
NSA_CMP_LEN = 32
NSA_CMP_STRIDE = 16
NSA_CMP_HIDDEN = 256
NSA_SEL_BLOCK = 64
NSA_TOPN = 16
NSA_WINDOW = 512
MOBA_HEADS = 4
MOBA_W = MOBA_HEADS * HEAD_DIM
MOBA_BLOCK = 256
MOBA_TOPK = 3
CONV_CH = 256
CONV_WIDTH = 31
HGRN_HEADS = 4
HGRN_KDIM = 64
HGRN_VDIM = 64
HGRN_W = HGRN_HEADS * HGRN_VDIM
HGRN_CHUNK = 64
MIX_WIDTH = NSA_W + MOBA_W + CONV_CH + HGRN_W
IN_WIDTHS = (NSA_W, HEAD_DIM, HEAD_DIM, HEAD_DIM, HEAD_DIM, HEAD_DIM, HEAD_DIM, 3 * NSA_HEADS,
             MOBA_W, MOBA_W, MOBA_W, 2 * CONV_CH, HGRN_W, HGRN_W, HGRN_W, HGRN_W)
D_FF = 4 * D_MODEL
REL_BUCKETS = 32
REL_MAX_EXACT = 16
REL_MAX_DIST = 128
TQ = 256
MOBA_TQ = 512
PAIR = 2
PREV = 128
LN_EPS = 1e-5
RMS_EPS = 1e-6
BIG = 1e9
DEEPNORM_ALPHA = (2 * DEPTH) ** 0.25

HF_COLS = 2304
COL_CONV = 0
COL_QN = 512
COL_QM = 768
COL_QH = 1024
COL_FH = 1280
COL_IH = 1536
COL_GH = 1792
COL_KVC = 2048
COL_GN = 2176
KV_COLS = 768
KCOL_KVS = 0
KCOL_KVW = 128
KCOL_KM = 256
KCOL_VM = 512
_SEG_DST = (COL_QN, COL_KVC, COL_KVC + HEAD_DIM, HF_COLS + KCOL_KVS, HF_COLS + KCOL_KVS + HEAD_DIM,
            HF_COLS + KCOL_KVW, HF_COLS + KCOL_KVW + HEAD_DIM, COL_GN, COL_QM, HF_COLS + KCOL_KM,
            HF_COLS + KCOL_VM, COL_CONV, COL_QH, COL_FH, COL_IH, COL_GH)

VMEM_LIMIT = 48 * 1024 * 1024
NEG = -1e30

_f32 = jnp.float32
_bf16 = jnp.bfloat16


def _cparams(sem):
    return pltpu.CompilerParams(dimension_semantics=sem, vmem_limit_bytes=VMEM_LIMIT)


def _layer_norm(z, g, b):
    mu = jnp.mean(z, axis=-1, keepdims=True)
    zc = z - mu
    var = jnp.mean(zc * zc, axis=-1, keepdims=True)
    return zc * lax.rsqrt(var + LN_EPS) * g + b


def _in_proj_kernel(x_ref, w_ref, o_ref, ob_ref):
    acc = jnp.dot(x_ref[...].astype(_bf16), w_ref[...], preferred_element_type=_f32)
    o_ref[...] = acc[:, :HF_COLS]
    ob_ref[...] = acc[:, HF_COLS:].astype(_bf16)


def in_proj(x2, w_bf16, tm=512):
    n, d = x2.shape
    assert w_bf16.shape[1] == HF_COLS + KV_COLS
    return pl.pallas_call(
        _in_proj_kernel,
        out_shape=(jax.ShapeDtypeStruct((n, HF_COLS), _f32), jax.ShapeDtypeStruct((n, KV_COLS), _bf16)),
        grid=(n // tm,),
        in_specs=[pl.BlockSpec((tm, d), lambda i: (i, 0)),
                  pl.BlockSpec((d, HF_COLS + KV_COLS), lambda i: (0, 0))],
        out_specs=(pl.BlockSpec((tm, HF_COLS), lambda i: (i, 0)), pl.BlockSpec((tm, KV_COLS), lambda i: (i, 0))),
        compiler_params=_cparams(("parallel",)),
        name="in_proj",
    )(x2, w_bf16)


def _regroup_w_in(w_in_l):
    dst_of_src = np.concatenate([d0 + np.arange(w) for d0, w in zip(_SEG_DST, IN_WIDTHS)]).astype(np.int32)
    assert dst_of_src.shape[0] == w_in_l.shape[1] and len(set(dst_of_src.tolist())) == dst_of_src.shape[0]
    dst = lax.broadcasted_iota(jnp.int32, (dst_of_src.shape[0], HF_COLS + KV_COLS), 1)
    place = (dst == jnp.asarray(dst_of_src)[:, None]).astype(_bf16)
    return jnp.dot(w_in_l.astype(_bf16), place, preferred_element_type=_f32).astype(_bf16)


def _out_proj_kernel(a_ref, b_ref, c_ref, d_ref, x_ref, w_ref, g_ref, beta_ref, o_ref):
    mixed = jnp.dot(a_ref[...], w_ref[0:256, :], preferred_element_type=_f32)
    mixed += jnp.dot(b_ref[...], w_ref[256:512, :], preferred_element_type=_f32)
    mixed += jnp.dot(c_ref[...], w_ref[512:768, :], preferred_element_type=_f32)
    mixed += jnp.dot(d_ref[...], w_ref[768:1024, :], preferred_element_type=_f32)
    z = DEEPNORM_ALPHA * x_ref[...] + mixed
    o_ref[...] = _layer_norm(z, g_ref[...], beta_ref[...])


def out_proj_ln(o_nsa, o_moba, o_conv, o_hgrn, x2, w_out_bf16, g, b, tm=512):
    n, d = x2.shape
    part = pl.BlockSpec((tm, 256), lambda i: (i, 0))
    vec = pl.BlockSpec((1, d), lambda i: (0, 0))
    return pl.pallas_call(
        _out_proj_kernel,
        out_shape=jax.ShapeDtypeStruct((n, d), _f32),
        grid=(n // tm,),
        in_specs=[part, part, part, part,
                  pl.BlockSpec((tm, d), lambda i: (i, 0)),
                  pl.BlockSpec((MIX_WIDTH, d), lambda i: (0, 0)),
                  vec, vec],
        out_specs=pl.BlockSpec((tm, d), lambda i: (i, 0)),
        compiler_params=_cparams(("parallel",)),
        name="out_proj_ln",
    )(o_nsa, o_moba, o_conv, o_hgrn, x2, w_out_bf16, g.reshape(1, d), b.reshape(1, d))


def _mlp_kernel(x_ref, w1_ref, w2_ref, g_ref, beta_ref, o_ref, acc_ref):
    j = pl.program_id(1)

    @pl.when(j == 0)
    def _():
        acc_ref[...] = jnp.zeros_like(acc_ref)

    hid = jnp.dot(x_ref[...].astype(_bf16), w1_ref[...], preferred_element_type=_f32)
    hid = jnp.square(jnp.maximum(hid, 0.0))
    acc_ref[...] += jnp.dot(hid.astype(_bf16), w2_ref[...], preferred_element_type=_f32)

    @pl.when(j == pl.num_programs(1) - 1)
    def _():
        z = DEEPNORM_ALPHA * x_ref[...] + acc_ref[...]
        o_ref[...] = _layer_norm(z, g_ref[...], beta_ref[...])


def mlp_ln(x1, w1_bf16, w2_bf16, g, b, tm=1024, tf=1024):
    n, d = x1.shape
    dff = w1_bf16.shape[1]
    vec = pl.BlockSpec((1, d), lambda i, j: (0, 0))
    return pl.pallas_call(
        _mlp_kernel,
        out_shape=jax.ShapeDtypeStruct((n, d), _f32),
        grid=(n // tm, dff // tf),
        in_specs=[pl.BlockSpec((tm, d), lambda i, j: (i, 0)),
                  pl.BlockSpec((d, tf), lambda i, j: (0, j)),
                  pl.BlockSpec((tf, d), lambda i, j: (j, 0)),
                  vec, vec],
        out_specs=pl.BlockSpec((tm, d), lambda i, j: (i, 0)),
        scratch_shapes=[pltpu.VMEM((tm, d), _f32)],
        compiler_params=_cparams(("parallel", "arbitrary")),
        name="mlp_ln",
    )(x1, w1_bf16, w2_bf16, g.reshape(1, d), b.reshape(1, d))


SUBLANES = 8
CONV_HALO = 32


def _conv_kernel(u_ref, wdw_ref, bdw_ref, g_ref, beta_ref, wpw_ref, o_ref, buf_ref, rot_ref):
    t = pl.program_id(1)
    tt = u_ref.shape[0]

    @pl.when(t == 0)
    def _():
        buf_ref[0:CONV_HALO, :] = jnp.zeros((CONV_HALO, CONV_CH), _f32)
        buf_ref[CONV_HALO + tt:, :] = jnp.zeros((SUBLANES, CONV_CH), _f32)

    @pl.when(t > 0)
    def _():
        buf_ref[0:CONV_HALO, :] = buf_ref[tt:tt + CONV_HALO, :]

    u = u_ref[...]
    a = u[:, :CONV_CH]
    gl = u[:, CONV_CH:]
    buf_ref[CONV_HALO:CONV_HALO + tt, :] = a * jax.nn.sigmoid(gl)

    acc = jnp.zeros((tt, CONV_CH), _f32) + bdw_ref[...]
    first = CONV_HALO - (CONV_WIDTH - 1)
    for r in range(SUBLANES):
        taps = [w for w in range(CONV_WIDTH) if (first + w) % SUBLANES == r]
        if not taps:
            continue
        rot_ref[r % 2] = buf_ref[r:r + CONV_HALO + tt, :]
        for w in taps:
            base = first + w - r
            acc = acc + rot_ref[r % 2, base:base + tt, :] * wdw_ref[w:w + 1, :]
    hn = _layer_norm(acc, g_ref[...], beta_ref[...])
    hn = hn * jax.nn.sigmoid(hn)
    o_ref[...] = jnp.dot(hn.astype(_bf16), wpw_ref[...], preferred_element_type=_f32).astype(o_ref.dtype)


def conformer_conv(h, batch, seq, w_dw, b_dw, ln_g, ln_b, w_pw_bf16, tt=512):
    nt = seq // tt
    vec = pl.BlockSpec((1, CONV_CH), lambda b, t: (0, 0))
    return pl.pallas_call(
        _conv_kernel,
        out_shape=jax.ShapeDtypeStruct((batch * seq, CONV_CH), _bf16),
        grid=(batch, nt),
        in_specs=[pl.BlockSpec((tt, 2 * CONV_CH), lambda b, t: (b * nt + t, COL_CONV // (2 * CONV_CH))),
                  pl.BlockSpec((CONV_WIDTH, CONV_CH), lambda b, t: (0, 0)),
                  vec, vec, vec,
                  pl.BlockSpec((CONV_CH, CONV_CH), lambda b, t: (0, 0))],
        out_specs=pl.BlockSpec((tt, CONV_CH), lambda b, t: (b * nt + t, 0)),
        scratch_shapes=[pltpu.VMEM((CONV_HALO + tt + SUBLANES, CONV_CH), _f32),
                        pltpu.VMEM((2, CONV_HALO + tt, CONV_CH), _f32)],
        compiler_params=_cparams(("parallel", "arbitrary")),
        name="conformer_conv",
    )(h, w_dw, b_dw.reshape(1, CONV_CH), ln_g.reshape(1, CONV_CH), ln_b.reshape(1, CONV_CH), w_pw_bf16)


HGRN_LEVELS = (32, 16, 8, 4, 2, 1)
HGRN_TILE = 256


def _hgrn_constants():
    c = HGRN_CHUNK
    t = np.arange(c)[:, None]
    u = np.arange(c)[None, :]
    masks = []
    for h in HGRN_LEVELS:
        mid = (t // (2 * h)) * (2 * h) + h
        upper = t >= mid
        same = (t // (2 * h)) == (u // (2 * h))
        mid_s = (u // (2 * h)) * (2 * h) + h
        masks.append(same & upper & (u < mid_s))
    masks.append(t == u)
    lmat = (u <= t).astype(np.float32)
    mask = np.stack(masks).astype(np.float32)
    mask = np.tile(mask, (1, HGRN_HEADS, 1))
    lane_head = np.arange(HGRN_W)[None, :] // HGRN_KDIM
    lane_mask = (lane_head == np.arange(HGRN_HEADS)[:, None]).astype(np.float32)
    bd = (lane_head.T == lane_head).astype(np.float32)
    return lmat, mask, lane_mask, bd


def _split3(x):
    hi = x.astype(_bf16)
    r = x - hi.astype(_f32)
    mid = r.astype(_bf16)
    lo = (r - mid.astype(_f32)).astype(_bf16)
    return hi, mid, lo


def _dot(a, b):
    return jnp.dot(a, b, preferred_element_type=_f32)


def _dot_nt(a, b):
    return lax.dot_general(a, b, (((1,), (1,)), ((), ())), preferred_element_type=_f32)


def _dot_tn(a, b):
    return lax.dot_general(a, b, (((0,), (0,)), ((), ())), preferred_element_type=_f32)


def _dot3(l_bf16, x):
    hi, mid, lo = _split3(x)
    return _dot(l_bf16, hi) + _dot(l_bf16, mid) + _dot(l_bf16, lo)


def _level_exponent(h, cum, logf, prev_g, next_g, row):
    c, w = cum.shape
    if h == 1:
        return jnp.where(row % 2 == 1, logf, 0.0)
    if h == 2:
        ph = row % 4
        return jnp.where(ph == 2, logf, jnp.where(ph == 3, logf + prev_g, jnp.where(ph == 0, next_g, 0.0)))
    pieces = [jnp.broadcast_to(cum[b * 2 * h + h - 1:b * 2 * h + h, :], (2 * h, w)) for b in range(c // (2 * h))]
    ref = pieces[0] if len(pieces) == 1 else jnp.concatenate(pieces, axis=0)
    return -jnp.abs(cum - ref)


def _hgrn_kernel(q_ref, f_ref, i_ref, g_ref, lb_ref, ng_ref, lmat_ref, mask_ref, lm_ref, bd_ref,
                 o_ref, st_ref):
    c = HGRN_CHUNK
    nl = len(HGRN_LEVELS)

    @pl.when(pl.program_id(1) == 0)
    def _():
        st_ref[...] = jnp.zeros_like(st_ref)

    lb = lb_ref[...]
    bd = bd_ref[...]
    row = lax.broadcasted_iota(jnp.int32, (c, 1), 0)

    def chunk(ci):
        rows = pl.ds(ci * c, c)
        q = q_ref[rows, :]
        f = lb + (1.0 - lb) * jax.nn.sigmoid(f_ref[rows, :])
        logf = jnp.log(f)
        k = 1.0 - f
        v = i_ref[rows, :]
        v_b = v.astype(_bf16)
        cum = _dot3(lmat_ref[...], logf)
        e_cum = jnp.exp(cum)
        e_rest = jnp.exp(cum[c - 1:c, :] - cum)
        prev_g = pltpu.roll(logf, 1, 0)
        next_g = pltpu.roll(logf, c - 1, 0)

        attn = jnp.zeros((HGRN_HEADS * c, c), _f32)
        for l in range(nl + 1):
            if l < nl:
                el = jnp.exp(_level_exponent(HGRN_LEVELS[l], cum, logf, prev_g, next_g, row))
                ql, kl = q * el, k * el
            else:
                ql, kl = q, k
            qs = jnp.concatenate([ql * lm_ref[h:h + 1, :] for h in range(HGRN_HEADS)], axis=0)
            attn = attn + _dot_nt(qs.astype(_bf16), kl.astype(_bf16)) * mask_ref[l]
        o = jnp.zeros((c, HGRN_W), _f32)
        for h in range(HGRN_HEADS):
            vh = (v * lm_ref[h:h + 1, :]).astype(_bf16)
            o = o + _dot(attn[h * c:(h + 1) * c, :].astype(_bf16), vh)

        st = st_ref[...]
        o = o + _dot_nt((q * e_cum).astype(_bf16), st.astype(_bf16))
        upd = _dot_tn(v_b, (k * e_rest).astype(_bf16)) * bd
        st_ref[...] = st * e_cum[c - 1:c, :] + upd

        osq = o * o
        hi = osq.astype(_bf16)
        lo = (osq - hi.astype(_f32)).astype(_bf16)
        bd_b = bd.astype(_bf16)
        ms = (_dot(hi, bd_b) + _dot(lo, bd_b)) * (1.0 / HGRN_VDIM)
        o = o * lax.rsqrt(ms + RMS_EPS)
        o_ref[rows, :] = (o * ng_ref[...] * jax.nn.sigmoid(g_ref[rows, :])).astype(o_ref.dtype)

    for ci in range(q_ref.shape[0] // c):
        chunk(ci)


def hgrn2(h, batch, seq, lb, norm_g):
    lmat, mask, lane_mask, bd = _hgrn_constants()
    tt = HGRN_TILE
    nt = seq // tt

    def col(cb):
        return pl.BlockSpec((tt, HGRN_W), lambda b, t: (b * nt + t, cb))

    def const(shape):
        nd = len(shape)
        return pl.BlockSpec(shape, lambda b, t: (0,) * nd)

    return pl.pallas_call(
        _hgrn_kernel,
        out_shape=jax.ShapeDtypeStruct((batch * seq, HGRN_W), _bf16),
        grid=(batch, nt),
        in_specs=[col(COL_QH // HGRN_W), col(COL_FH // HGRN_W), col(COL_IH // HGRN_W), col(COL_GH // HGRN_W),
                  const((1, HGRN_W)), const((1, HGRN_W)),
                  const(lmat.shape), const(mask.shape), const(lane_mask.shape), const(bd.shape)],
        out_specs=pl.BlockSpec((tt, HGRN_W), lambda b, t: (b * nt + t, 0)),
        scratch_shapes=[pltpu.VMEM((HGRN_W, HGRN_W), _f32)],
        compiler_params=_cparams(("parallel", "arbitrary")),
        name="hgrn2",
    )(h, h, h, h, lb.reshape(1, HGRN_W), norm_g.reshape(1, HGRN_W),
      jnp.asarray(lmat, _bf16), jnp.asarray(mask), jnp.asarray(lane_mask), jnp.asarray(bd))


def _rel_bucket_np(dist):
    n = np.maximum(dist, 0)
    nf = np.maximum(n, 1).astype(np.float32)
    large = REL_MAX_EXACT + (np.log(nf / np.float32(REL_MAX_EXACT)) / np.float32(math.log(REL_MAX_DIST / REL_MAX_EXACT))
                             * np.float32(REL_BUCKETS - REL_MAX_EXACT)).astype(np.int32)
    return np.where(n < REL_MAX_EXACT, n, np.minimum(large, REL_BUCKETS - 1)).astype(np.int32)


FAR_DIST = 113
assert int(_rel_bucket_np(np.arange(FAR_DIST, 4 * FAR_DIST)).min()) == REL_BUCKETS - 1


LOG2E = 1.4426950408889634
M_FLOOR = -5e29


def _bias_rel(tab, dist):
    dist = np.asarray(dist)
    vals = jnp.take(tab, jnp.asarray(_rel_bucket_np(dist).reshape(-1)), axis=1)
    vals = (vals - tab[:, REL_BUCKETS - 1:]) * LOG2E
    vals = vals.reshape((tab.shape[0],) + dist.shape)
    return jnp.where(jnp.asarray(dist >= 0)[None], vals, NEG)


def _toeplitz_kernel(g_ref, o_ref):
    rows, width = o_ref.shape
    x = jnp.broadcast_to(g_ref[...], (rows, g_ref.shape[1]))
    o_ref[...] = pltpu.roll(x, 0, 1, stride=1, stride_axis=0)[:, :width]


def _toeplitz(g, rows, width):
    nh, length = g.shape
    return pl.pallas_call(
        _toeplitz_kernel,
        out_shape=jax.ShapeDtypeStruct((nh, rows, width), _f32),
        grid=(nh,),
        in_specs=[pl.BlockSpec((None, 1, length), lambda h: (h, 0, 0))],
        out_specs=pl.BlockSpec((None, rows, width), lambda h: (h, 0, 0)),
        compiler_params=_cparams(("parallel",)),
        name="toeplitz_bias",
    )(g.reshape(nh, 1, length))


def _near_bias(tab, tq):
    k = np.arange(2 * (PREV + tq))
    g = _bias_rel(tab, np.where(k < PREV + tq, PREV - k, 4 * FAR_DIST))
    return _toeplitz(g, tq, PREV + tq)


FAR_TILE = 512
FAR_UNROLL = 4
MOBA_UNROLL = 4


def _softmax_update(m, acc, s, rhs_b):
    m_new = jnp.maximum(m, jnp.max(s, axis=-1, keepdims=True))
    alpha = jnp.exp2(m - m_new)
    p = jnp.exp2(s - m_new)
    return m_new, alpha * acc + _dot(p.astype(_bf16), rhs_b)


def _far_sweep(n_tiles, tile_step, init, unroll, exact_tail):
    def body(it, carry):
        for u in range(unroll):
            carry = tile_step(it * unroll + u, carry)
        return carry

    if not exact_tail:
        return lax.fori_loop(0, (n_tiles + unroll - 1) // unroll, body, init)
    n_main = n_tiles // unroll
    carry = lax.fori_loop(0, n_main, body, init)
    return lax.fori_loop(n_main * unroll, n_tiles, tile_step, carry)


def _block_onehot_neg(seq, block):
    k = lax.broadcasted_iota(jnp.int32, (seq, 128), 0)
    lane = lax.broadcasted_iota(jnp.int32, (seq, 128), 1)
    return jnp.where(lane == k // block, NEG, 0.0).astype(_bf16)


MOBA_UNIT = PREV


def _moba_kernel(q_ref, k_ref, v_ref, et_ref, bn_ref, o_ref, kmean_ref):
    c = pl.program_id(2)
    seq = k_ref.shape[1]
    nb = seq // MOBA_BLOCK
    tq = MOBA_TQ

    @pl.when(c == 0)
    def _():
        for g in range(PAIR):
            kmean_ref[g] = jnp.sum(k_ref[g].astype(_f32).reshape(nb, MOBA_BLOCK, 128), axis=1) * (1.0 / MOBA_BLOCK)

    s0 = c * tq
    far_end = s0 - PREV
    n_far = (jnp.maximum(far_end, 0) + FAR_TILE - 1) // FAR_TILE

    lane = lax.broadcasted_iota(jnp.int32, (1, 128), 1)
    jcol = lax.broadcasted_iota(jnp.int32, (nb, tq), 0)
    own = (lax.broadcasted_iota(jnp.int32, (nb, tq), 1) + s0) // MOBA_BLOCK
    rep = (lax.broadcasted_iota(jnp.int32, (nb, 128), 1) // (MOBA_BLOCK // MOBA_UNIT)
           == lax.broadcasted_iota(jnp.int32, (nb, 128), 0)).astype(_bf16)

    qfs, qns = [], []
    for g in range(PAIR):
        q_all = q_ref[g] * (HEAD_DIM ** -0.5 * LOG2E)
        kmean_b = kmean_ref[g].astype(_bf16)
        q_far, q_near = [], []
        for hh in range(2):
            qh = jnp.where(lane // HEAD_DIM == hh, q_all, 0.0).astype(_bf16)
            gate = _dot_nt(kmean_b, qh)
            work = jnp.where(jcol < own, gate, -BIG)
            sel = jnp.zeros((nb, tq), _f32)
            for _ in range(MOBA_TOPK):
                mx = jnp.max(work, axis=0, keepdims=True)
                idx = jnp.min(jnp.where(work == mx, jcol, nb), axis=0, keepdims=True)
                pick = jcol == idx
                sel = jnp.where(pick & (mx > -0.5 * BIG * LOG2E), 1.0, sel)
                work = jnp.where(pick, -jnp.inf, work)
            sel = jnp.where(jcol == own, 1.0, sel)
            notsel = _dot_tn((1.0 - sel).astype(_bf16), rep)
            notsel_far = jnp.where(lane < far_end // MOBA_UNIT, notsel, 1.0)
            q_far.append(jnp.concatenate([qh, notsel_far.astype(_bf16)], axis=1))
            q_near.append(jnp.concatenate([qh, notsel.astype(_bf16)], axis=1))
        qfs.append(jnp.concatenate(q_far, axis=0))
        qns.append(jnp.concatenate(q_near, axis=0))
    ones = jnp.ones((max(FAR_TILE, PREV + tq), 128), _bf16)

    def far_tile(kt, carry):
        rows = pl.ds(pl.multiple_of(kt * FAR_TILE, FAR_TILE), FAR_TILE)
        et = et_ref[rows, :]
        out = []
        for g in range(PAIR):
            s = _dot_nt(qfs[g], jnp.concatenate([k_ref[g, rows, :], et], axis=1))
            out.append(_softmax_update(*carry[g], s, jnp.concatenate([v_ref[g, rows, :], ones[:FAR_TILE]], axis=1)))
        return tuple(out)

    init = tuple((jnp.full((2 * tq, 1), M_FLOOR, _f32), jnp.zeros((2 * tq, 256), _f32)) for _ in range(PAIR))
    state = _far_sweep(n_far, far_tile, init, MOBA_UNROLL, exact_tail=False)

    prev = pl.ds(pl.multiple_of(jnp.maximum(far_end, 0), PREV), PREV)
    diag = pl.ds(pl.multiple_of(s0, tq), tq)
    col = lax.broadcasted_iota(jnp.int32, (1, 1, PREV + tq), 2)
    nbias = jnp.where((col < PREV) & (c == 0), NEG, bn_ref[...])
    nbias = nbias.reshape(2 * tq, PREV + tq)
    et_n = jnp.concatenate([et_ref[prev, :], et_ref[diag, :]], axis=0)
    for g in range(PAIR):
        k_n = jnp.concatenate([jnp.concatenate([k_ref[g, prev, :], k_ref[g, diag, :]], axis=0), et_n], axis=1)
        v_n = jnp.concatenate([jnp.concatenate([v_ref[g, prev, :], v_ref[g, diag, :]], axis=0), ones[:PREV + tq]],
                              axis=1)
        _, acc = _softmax_update(*state[g], _dot_nt(qns[g], k_n) + nbias, v_n)
        out = acc[:, :128] / acc[:, 128:]
        o_ref[g] = jnp.where(lane // HEAD_DIM == 0, out[:tq], out[tq:]).astype(o_ref.dtype)


def moba_attention(h, hkv, batch, seq, bias_near):
    tq = MOBA_TQ
    nq = seq // tq
    assert seq // MOBA_UNIT <= 128 and tq % MOBA_BLOCK == 0 and batch % PAIR == 0
    et = _block_onehot_neg(seq, MOBA_UNIT)
    h3 = h.reshape(batch, seq, HF_COLS)
    hkv3 = hkv.reshape(batch, seq, KV_COLS)

    def kv(col):
        return pl.BlockSpec((PAIR, seq, 128), lambda b, hp, c: (b, 0, col // 128 + hp))

    out = pl.pallas_call(
        _moba_kernel,
        out_shape=jax.ShapeDtypeStruct((batch, seq, MOBA_W), _bf16),
        grid=(batch // PAIR, MOBA_HEADS // 2, nq),
        in_specs=[pl.BlockSpec((PAIR, tq, 128), lambda b, hp, c: (b, c, COL_QM // 128 + hp)),
                  kv(KCOL_KM), kv(KCOL_VM),
                  pl.BlockSpec((seq, 128), lambda b, hp, c: (0, 0)),
                  pl.BlockSpec((2, tq, PREV + tq), lambda b, hp, c: (hp, 0, 0))],
        out_specs=pl.BlockSpec((PAIR, tq, 128), lambda b, hp, c: (b, c, hp)),
        scratch_shapes=[pltpu.VMEM((PAIR, seq // MOBA_BLOCK, 128), _f32)],
        compiler_params=_cparams(("parallel", "parallel", "arbitrary")),
        name="moba_attention",
    )(h3, hkv3, hkv3, et, bias_near)
    return out.reshape(batch * seq, MOBA_W)


def _compress_kernel(a_ref, pet_ref, peb_ref, wt_ref, wb_ref, b1_ref, w2_ref, o_ref):
    a = a_ref[...]
    nrow = a.shape[0]
    top = _dot((a + pet_ref[...]).astype(_bf16), wt_ref[...])
    bot = _dot((a + peb_ref[...]).astype(_bf16), wb_ref[...])
    pre = top + pltpu.roll(bot, nrow - 1, 0) + b1_ref[...]
    hid = jax.nn.gelu(pre)
    o_ref[...] = _dot(hid.astype(_bf16), w2_ref[...])


def nsa_compress(h, batch, seq, pe, w1, b1, w2):
    nr = seq // NSA_CMP_STRIDE
    half = NSA_CMP_LEN // 2
    a = h[:, COL_KVC:COL_KVC + 128].reshape(batch, nr, NSA_CMP_STRIDE * 128)
    pe_cat = jnp.concatenate([pe[0], pe[1]], axis=-1)
    pe_top = pe_cat[:half].reshape(1, half * 128)
    pe_bot = pe_cat[half:].reshape(1, half * 128)
    hid = NSA_CMP_HIDDEN
    w_all = jnp.zeros((NSA_CMP_LEN, 128, 2 * hid), _f32)
    w_all = w_all.at[:, :HEAD_DIM, :hid].set(w1[0].reshape(NSA_CMP_LEN, HEAD_DIM, hid))
    w_all = w_all.at[:, HEAD_DIM:, hid:].set(w1[1].reshape(NSA_CMP_LEN, HEAD_DIM, hid))
    w_top = w_all[:half].reshape(half * 128, 2 * hid).astype(_bf16)
    w_bot = w_all[half:].reshape(half * 128, 2 * hid).astype(_bf16)
    b1cat = jnp.concatenate([b1[0], b1[1]]).reshape(1, 2 * hid)
    w2bd = jnp.zeros((2 * hid, 128), _f32)
    w2bd = w2bd.at[:hid, :HEAD_DIM].set(w2[0]).at[hid:, HEAD_DIM:].set(w2[1]).astype(_bf16)

    def const(shape):
        return pl.BlockSpec(shape, lambda b: (0,) * len(shape))

    return pl.pallas_call(
        _compress_kernel,
        out_shape=jax.ShapeDtypeStruct((batch, nr, 128), _f32),
        grid=(batch,),
        in_specs=[pl.BlockSpec((None, nr, half * 128), lambda b: (b, 0, 0)),
                  const(pe_top.shape), const(pe_bot.shape), const(w_top.shape), const(w_bot.shape),
                  const(b1cat.shape), const(w2bd.shape)],
        out_specs=pl.BlockSpec((None, nr, 128), lambda b: (b, 0, 0)),
        compiler_params=_cparams(("parallel",)),
        name="nsa_compress",
    )(a, pe_top, pe_bot, w_top, w_bot, b1cat, w2bd)


NSA_FORCED = 3
assert NSA_TOPN >= NSA_FORCED
CMP_FIRST = PREV // NSA_CMP_STRIDE
CMP_NEAR = -(-((TQ + PREV - NSA_CMP_LEN) // NSA_CMP_STRIDE + 1) // 8) * 8
NSA_WKEYS = NSA_WINDOW + TQ
NSA_WTILES = NSA_WKEYS // PREV


def _nsa_tables(tab, seq):
    nc = seq // NSA_CMP_STRIDE
    nsel = seq // NSA_SEL_BLOCK
    n_cmp = (seq - NSA_CMP_LEN) // NSA_CMP_STRIDE + 1
    i = np.arange(TQ)[:, None]
    m = np.arange(CMP_NEAR)[None, :]
    d_c = i + (PREV - NSA_CMP_LEN + 1) - NSA_CMP_STRIDE * m
    cmp_bias = _bias_rel(tab, np.maximum(d_c, 0))
    k = np.arange(2 * NSA_WKEYS)
    g = _bias_rel(tab, np.where((k >= 1) & (k <= NSA_WINDOW), NSA_WINDOW - k, -1))
    win_bias = _toeplitz(g, TQ, NSA_WKEYS)
    n = np.arange(nc)[:, None]
    js = np.arange(nsel)[None, :]
    cs, ce = n * NSA_CMP_STRIDE, n * NSA_CMP_STRIDE + NSA_CMP_LEN
    ss = js * NSA_SEL_BLOCK
    ov = np.clip(np.minimum(ce, ss + NSA_SEL_BLOCK) - np.maximum(cs, ss), 0, None).astype(np.float32) / NSA_CMP_LEN
    ov = np.where(n < n_cmp, ov, 0.0)
    return cmp_bias, win_bias, _near_bias(tab, TQ), jnp.asarray(ov.T, _bf16)


def _exact_dot_r(x, r_bf16):
    hi = x.astype(_bf16)
    lo = (x - hi.astype(_f32)).astype(_bf16)
    return _dot(hi, r_bf16) + _dot(lo, r_bf16)


def _nsa_kernel(q_ref, gn_ref, kvc_ref, kvs_ref, kvw_ref, et_ref, cb_ref, wb_ref, nb_ref, ovt_ref, o_ref):
    c = pl.program_id(1)
    tq = TQ
    nh = NSA_HEADS
    nc = kvc_ref.shape[0]
    nsel = ovt_ref.shape[0]
    s0 = c * tq
    far_end = s0 - PREV
    lane = lax.broadcasted_iota(jnp.int32, (1, 128), 1)
    low = lane < HEAD_DIM

    q_all = q_ref[...] * (HEAD_DIM ** -0.5 * LOG2E)
    qh = []
    for h in range(nh):
        blk = q_all[:, 128 * (h // 2):128 * (h // 2) + 128]
        if h % 2 == 1:
            blk = pltpu.roll(blk, HEAD_DIM, 1)
        qh.append(jnp.where(low, blk, 0.0))
    q_b = jnp.concatenate(qh, axis=0).astype(_bf16)

    def ones_v(kv):
        return jnp.where(low, 1.0, kv).astype(_bf16)

    kvc_b = kvc_ref[...].astype(_bf16)
    s_c = _dot_nt(q_b, kvc_b).reshape(nh, tq, nc)
    n_idx = lax.broadcasted_iota(jnp.int32, (tq, nc), 1)
    t_idx = lax.broadcasted_iota(jnp.int32, (tq, nc), 0) + s0
    mask_c = jnp.where((n_idx * NSA_CMP_STRIDE + (NSA_CMP_LEN - 1)) <= t_idx, 0.0, NEG)
    m_sel = lax.broadcasted_iota(jnp.int32, (CMP_NEAR, nc), 0)
    n_sel = lax.broadcasted_iota(jnp.int32, (CMP_NEAR, nc), 1)
    place = (n_sel == (s0 // NSA_CMP_STRIDE) - CMP_FIRST + m_sel).astype(_bf16)
    bias_c = _exact_dot_r(cb_ref[...].reshape(nh * tq, CMP_NEAR), place).reshape(nh, tq, nc)
    psum = jnp.zeros((tq, nc), _f32)
    o_c = []
    for h in range(nh):
        s = s_c[h] + bias_c[h] + mask_c
        mx = jnp.maximum(jnp.max(s, axis=1, keepdims=True), M_FLOOR)
        e = jnp.exp2(s - mx)
        p = e * (1.0 / jnp.maximum(jnp.sum(e, axis=1, keepdims=True), 1e-30))
        psum = psum + p
        o_c.append(_dot(p.astype(_bf16), kvc_b))

    imp_t = jnp.zeros((nsel, tq), _f32)
    for part in _split3(psum):
        imp_t = imp_t + _dot_nt(ovt_ref[...], part)
    jblk = lax.broadcasted_iota(jnp.int32, (nsel, tq), 0)
    blk_t = (lax.broadcasted_iota(jnp.int32, (nsel, tq), 1) + s0) // NSA_SEL_BLOCK
    forced = (jblk == 0) | (jblk == blk_t) | (jblk == blk_t - 1)
    work = jnp.where((jblk <= blk_t) & jnp.logical_not(forced), imp_t, -BIG)
    for _ in range(min(NSA_TOPN, nsel) - NSA_FORCED):
        mx = jnp.max(work, axis=0, keepdims=True)
        idx = jnp.min(jnp.where(work == mx, jblk, nsel), axis=0, keepdims=True)
        work = jnp.where(jblk == idx, -jnp.inf, work)
    sel = jnp.where(forced | (work == -jnp.inf), 1.0, 0.0)

    eye = (lax.broadcasted_iota(jnp.int32, (nsel, 128), 0)
           == lax.broadcasted_iota(jnp.int32, (nsel, 128), 1)).astype(_bf16)
    notsel = _dot_tn((1.0 - sel).astype(_bf16), eye)
    notsel_far = jnp.where(lane < far_end // NSA_SEL_BLOCK, notsel, 1.0)
    qf = jnp.concatenate([q_b, jnp.concatenate([notsel_far.astype(_bf16)] * nh, axis=0)], axis=1)
    qn = jnp.concatenate([q_b, jnp.concatenate([notsel.astype(_bf16)] * nh, axis=0)], axis=1)

    def far_tile(kt, carry):
        rows = pl.ds(pl.multiple_of(kt * FAR_TILE, FAR_TILE), FAR_TILE)
        kv = kvs_ref[rows, :]
        s = _dot_nt(qf, jnp.concatenate([kv, et_ref[rows, :]], axis=1))
        return _softmax_update(*carry, s, ones_v(kv))

    n_far = (jnp.maximum(far_end, 0) + FAR_TILE - 1) // FAR_TILE
    init = (jnp.full((nh * tq, 1), M_FLOOR, _f32), jnp.zeros((nh * tq, 128), _f32))
    m, acc = _far_sweep(n_far, far_tile, init, FAR_UNROLL, exact_tail=True)

    prev = pl.ds(pl.multiple_of(jnp.maximum(far_end, 0), PREV), PREV)
    diag = pl.ds(pl.multiple_of(s0, tq), tq)
    kv = jnp.concatenate([kvs_ref[prev, :], kvs_ref[diag, :]], axis=0)
    et = jnp.concatenate([et_ref[prev, :], et_ref[diag, :]], axis=0)
    col = lax.broadcasted_iota(jnp.int32, (1, 1, PREV + tq), 2)
    nbias = jnp.where((col < PREV) & (c == 0), NEG, nb_ref[...])
    s = _dot_nt(qn, jnp.concatenate([kv, et], axis=1)) + nbias.reshape(nh * tq, PREV + tq)
    _, acc = _softmax_update(m, acc, s, ones_v(kv))
    o_s = (acc / acc[:, 0:1]).reshape(nh, tq, 128)

    tiles = []
    for r in range(NSA_WTILES):
        start = pl.multiple_of(jnp.maximum(s0 - NSA_WINDOW + r * PREV, 0), PREV)
        tiles.append(kvw_ref[pl.ds(start, PREV), :])
    kvw_b = jnp.concatenate(tiles, axis=0)
    wk = NSA_WKEYS
    kpos_w = lax.broadcasted_iota(jnp.int32, (1, 1, wk), 2) + (s0 - NSA_WINDOW)
    s = _dot_nt(q_b, kvw_b).reshape(nh, tq, wk) + jnp.where(kpos_w >= 0, wb_ref[...], NEG)
    e = jnp.exp2(s - jnp.max(s, axis=2, keepdims=True))
    acc_w = _dot(e.reshape(nh * tq, wk).astype(_bf16), ones_v(kvw_b))
    o_w = (acc_w / acc_w[:, 0:1]).reshape(nh, tq, 128)

    g = jax.nn.sigmoid(gn_ref[...])
    outs = []
    for h in range(nh):
        outs.append(g[:, 3 * h:3 * h + 1] * o_c[h] + g[:, 3 * h + 1:3 * h + 2] * o_s[h]
                    + g[:, 3 * h + 2:3 * h + 3] * o_w[h])
    for hp in range(nh // 2):
        o_ref[:, 128 * hp:128 * hp + 128] = jnp.where(low, pltpu.roll(outs[2 * hp], HEAD_DIM, 1),
                                                      outs[2 * hp + 1]).astype(o_ref.dtype)


def nsa_attention(h, hkv, kvc, batch, seq, tables):
    nq = seq // TQ
    nc = seq // NSA_CMP_STRIDE
    cmp_bias, win_bias, near_bias, ov_t = tables
    assert seq // NSA_SEL_BLOCK <= 128
    et = _block_onehot_neg(seq, NSA_SEL_BLOCK)

    def const(shape):
        return pl.BlockSpec(shape, lambda b, c: (0,) * len(shape))

    return pl.pallas_call(
        _nsa_kernel,
        out_shape=jax.ShapeDtypeStruct((batch * seq, NSA_W), _bf16),
        grid=(batch, nq),
        in_specs=[pl.BlockSpec((TQ, NSA_W), lambda b, c: (b * nq + c, COL_QN // NSA_W)),
                  pl.BlockSpec((TQ, 128), lambda b, c: (b * nq + c, COL_GN // 128)),
                  pl.BlockSpec((None, nc, 128), lambda b, c: (b, 0, 0)),
                  pl.BlockSpec((seq, 128), lambda b, c: (b, KCOL_KVS // 128)),
                  pl.BlockSpec((seq, 128), lambda b, c: (b, KCOL_KVW // 128)),
                  const(et.shape),
                  const(cmp_bias.shape), const(win_bias.shape), const(near_bias.shape), const(ov_t.shape)],
        out_specs=pl.BlockSpec((TQ, NSA_W), lambda b, c: (b * nq + c, 0)),
        compiler_params=_cparams(("parallel", "arbitrary")),
        name="nsa_attention",
    )(h, h, kvc, hkv, hkv, et, cmp_bias, win_bias, near_bias, ov_t)


def kernel(x, w_in, nsa_cmp_pe, nsa_cmp_w1, nsa_cmp_b1, nsa_cmp_w2, conv_dw_w, conv_dw_b, conv_ln_g,
           conv_ln_b, conv_pw_w, hgrn_lb_logits, hgrn_norm_g, w_out, ln1_g, ln1_b, w_ff1, w_ff2,
           ln2_g, ln2_b, rel_bias):
    batch, seq, d = x.shape
    lb_sm = jax.nn.softmax(hgrn_lb_logits.astype(_f32), axis=0)
    lbs = jnp.cumsum(lb_sm, axis=0) - lb_sm[0]
    x2 = x.reshape(batch * seq, d)
    nsa_tables = _nsa_tables(rel_bias[:NSA_HEADS], seq)
    moba_near = _near_bias(rel_bias[NSA_HEADS:], MOBA_TQ)
    for l in range(DEPTH):
        h, hkv = in_proj(x2, _regroup_w_in(w_in[l]))
        o_conv = conformer_conv(h, batch, seq, conv_dw_w[l], conv_dw_b[l], conv_ln_g[l], conv_ln_b[l],
                                conv_pw_w[l].astype(_bf16))
        o_hgrn = hgrn2(h, batch, seq, lbs[l], hgrn_norm_g[l])
        kvc = nsa_compress(h, batch, seq, nsa_cmp_pe[l], nsa_cmp_w1[l], nsa_cmp_b1[l], nsa_cmp_w2[l])
        o_nsa = nsa_attention(h, hkv, kvc, batch, seq, nsa_tables)
        o_moba = moba_attention(h, hkv, batch, seq, moba_near)
        x1 = out_proj_ln(o_nsa, o_moba, o_conv, o_hgrn, x2, w_out[l].astype(_bf16), ln1_g[l], ln1_b[l])
        x2 = mlp_ln(x1, w_ff1[l].astype(_bf16), w_ff2[l].astype(_bf16), ln2_g[l], ln2_b[l])
    return x2.reshape(batch, seq, d)
```
